```python
import jax, jax.numpy as jnp
from jax import lax
import numpy as np

D_MODEL = 1024
BATCH = 8
SEQ = 4096
DEPTH = 4

D_FF = 2816
NORM_EPS = 1e-6
POOL_WINDOWS = (2, 4, 8, 16)
POOL_GROUP_DIM = D_MODEL // 8
POOL_GROUPS = len(POOL_WINDOWS)
POOL_WIDTH = POOL_GROUPS * POOL_GROUP_DIM
SGU_GROUPS = 4
SGU_GROUP_DIM = D_MODEL // 8
SGU_WIDTH = SGU_GROUPS * SGU_GROUP_DIM
SGU_CHUNK = 128
AB_IN_WIDTH = POOL_WIDTH + 2 * SGU_WIDTH
AB_OUT_WIDTH = POOL_WIDTH + SGU_WIDTH
N_HEADS = 8
HEAD_DIM = D_MODEL // N_HEADS
ROT_DIM = HEAD_DIM // 4
ROPE_THETA = 500000.0
MOBA_BLOCK = 256
MOBA_TOPK = 3
QUERY_CHUNK = 16
NEG_INF = -1e30
N_EVEN = (DEPTH + 1) // 2
N_ODD = DEPTH // 2

kernel_name = 'hybrid_pool_sgu_moba_macaron'


def rms_norm(x, g):
    xf = x.astype(jnp.float32)
    y = xf * lax.rsqrt(jnp.mean(xf * xf, axis=-1, keepdims=True) + NORM_EPS)
    return (y * g.astype(jnp.float32)).astype(x.dtype)


def swiglu(h, w_gate, w_up, w_down):
    return (jax.nn.silu(h @ w_gate) * (h @ w_up)) @ w_down


def pool_mixer(a, w, scale):
    S = a.shape[1]
    af = a.astype(jnp.float32)
    cs0 = jnp.pad(jnp.cumsum(af, axis=1), ((0, 0), (1, 0), (0, 0), (0, 0)))
    t = jnp.arange(S)
    pooled = []
    for g, win in enumerate(POOL_WINDOWS):
        lo_idx = jnp.maximum(t + 1 - win, 0)
        win_sum = cs0[:, 1:, g] - cs0[:, lo_idx, g]
        count = jnp.minimum(t + 1, win).astype(jnp.float32)
        pooled.append(win_sum / count[None, :, None])
    d = (jnp.stack(pooled, axis=2) - af).astype(a.dtype)
    return jnp.einsum('bsgc,gcd->bsgd', d, w) * scale


def sgu_mixer(u, v, norm_g, w_s, b_s):
    B, S, G, C = u.shape
    u = jax.nn.gelu(u, approximate=False)
    v = rms_norm(jax.nn.gelu(v, approximate=False), norm_g)
    vc = v.reshape(B, S // SGU_CHUNK, SGU_CHUNK, G, C)
    causal = jnp.tril(jnp.ones((SGU_CHUNK, SGU_CHUNK), dtype=bool))
    w = jnp.where(causal[None], w_s, jnp.zeros_like(w_s))
    s = jnp.einsum('gts,bnsgc->bntgc', w, vc) + jnp.transpose(b_s)[:, :, None]
    return u * s.reshape(B, S, G, C)


def pool_sgu_mixer(h, w_in, pool_w, pool_scale, sgu_norm, sgu_w, sgu_b, w_out):
    B, S, _ = h.shape
    z = h @ w_in
    a = z[..., :POOL_WIDTH].reshape(B, S, POOL_GROUPS, POOL_GROUP_DIM)
    u = z[..., POOL_WIDTH:POOL_WIDTH + SGU_WIDTH].reshape(B, S, SGU_GROUPS, SGU_GROUP_DIM)
    v = z[..., POOL_WIDTH + SGU_WIDTH:].reshape(B, S, SGU_GROUPS, SGU_GROUP_DIM)
    y_a = pool_mixer(a, pool_w, pool_scale).reshape(B, S, POOL_WIDTH)
    y_b = sgu_mixer(u, v, sgu_norm, sgu_w, sgu_b).reshape(B, S, SGU_WIDTH)
    return jnp.concatenate([y_a, y_b], axis=-1) @ w_out


def partial_rotary(x, cos, sin):
    half = ROT_DIM // 2
    xf = x[..., :ROT_DIM].astype(jnp.float32)
    x1, x2 = xf[..., :half], xf[..., half:]
    rot = jnp.concatenate([x1 * cos - x2 * sin, x2 * cos + x1 * sin], axis=-1).astype(x.dtype)
    return jnp.concatenate([rot, x[..., ROT_DIM:]], axis=-1)


def moba_mixer(h, w_qkv, w_o):
    B, S, _ = h.shape
    qkv = (h @ w_qkv).reshape(B, S, 3, N_HEADS, HEAD_DIM).transpose(2, 0, 3, 1, 4)
    q, k, v = qkv[0], qkv[1], qkv[2]
    pos = jnp.arange(S, dtype=jnp.float32)
    inv_freq = 1.0 / (ROPE_THETA ** (jnp.arange(0, ROT_DIM, 2, dtype=jnp.float32) / ROT_DIM))
    ang = pos[:, None] * inv_freq[None, :]
    cos, sin = jnp.cos(ang), jnp.sin(ang)
    q = partial_rotary(q, cos, sin)
    k = partial_rotary(k, cos, sin)
    n_blocks = -(-S // MOBA_BLOCK)
    pad = n_blocks * MOBA_BLOCK - S
    kb = jnp.pad(k, ((0, 0), (0, 0), (0, pad), (0, 0))).reshape(B, N_HEADS, n_blocks, MOBA_BLOCK, HEAD_DIM)
    vb = jnp.pad(v, ((0, 0), (0, 0), (0, pad), (0, 0))).reshape(B, N_HEADS, n_blocks, MOBA_BLOCK, HEAD_DIM)
    kmean = jnp.mean(kb.astype(jnp.float32), axis=3)
    n_sel = min(MOBA_TOPK, n_blocks)
    scale = HEAD_DIM ** -0.5
    b_idx = jnp.arange(B)[:, None, None, None]
    h_idx = jnp.arange(N_HEADS)[None, :, None, None]
    block_ids = jnp.arange(n_blocks)

    def attend_chunk(c):
        start = c * QUERY_CHUNK
        qc = lax.dynamic_slice_in_dim(q, start, QUERY_CHUNK, axis=2)
        q_pos = start + jnp.arange(QUERY_CHUNK)
        cur = start // MOBA_BLOCK
        gate = jnp.einsum('bhqd,bhnd->bhqn', qc.astype(jnp.float32), kmean)
        gate = jnp.where(block_ids < cur, gate, NEG_INF)
        _, sel = lax.top_k(gate, n_sel)
        sel_valid = sel < cur
        kg = kb[b_idx, h_idx, sel]
        vg = vb[b_idx, h_idx, sel]
        s_past = jnp.einsum('bhqd,bhqknd->bhqkn', qc, kg).astype(jnp.float32) * scale
        s_past = jnp.where(sel_valid[..., None], s_past, NEG_INF)
        k_own = lax.dynamic_index_in_dim(kb, cur, axis=2, keepdims=False)
        v_own = lax.dynamic_index_in_dim(vb, cur, axis=2, keepdims=False)
        key_pos = cur * MOBA_BLOCK + jnp.arange(MOBA_BLOCK)
        s_own = jnp.einsum('bhqd,bhnd->bhqn', qc, k_own).astype(jnp.float32) * scale
        s_own = jnp.where(key_pos[None, :] <= q_pos[:, None], s_own, NEG_INF)
        scores = jnp.concatenate([s_past.reshape(B, N_HEADS, QUERY_CHUNK, n_sel * MOBA_BLOCK), s_own], axis=-1)
        p = jax.nn.softmax(scores, axis=-1).astype(v.dtype)
        p_past = p[..., :n_sel * MOBA_BLOCK].reshape(B, N_HEADS, QUERY_CHUNK, n_sel, MOBA_BLOCK)
        p_own = p[..., n_sel * MOBA_BLOCK:]
        return (jnp.einsum('bhqkn,bhqknd->bhqd', p_past, vg)
                + jnp.einsum('bhqn,bhnd->bhqd', p_own, v_own))

    o = lax.map(attend_chunk, jnp.arange(S // QUERY_CHUNK))
    o = jnp.transpose(o, (1, 0, 3, 2, 4)).reshape(B, S, N_HEADS * HEAD_DIM)
    return o @ w_o


def setup_inputs(seed: int = 0) -> dict:
    key = jax.random.key(seed)
    ks = jax.random.split(key, 20)
    f32 = jnp.float32

    def nrm(k, shape, fan_in):
        return jax.random.normal(k, shape, f32) * (fan_in ** -0.5)

    return {
        'x': jax.random.normal(ks[0], (BATCH, SEQ, D_MODEL), f32),
        'ffn1_w_gate': nrm(ks[1], (DEPTH, D_MODEL, D_FF), D_MODEL),
        'ffn1_w_up': nrm(ks[2], (DEPTH, D_MODEL, D_FF), D_MODEL),
        'ffn1_w_down': nrm(ks[3], (DEPTH, D_FF, D_MODEL), D_FF),
        'ffn2_w_gate': nrm(ks[4], (DEPTH, D_MODEL, D_FF), D_MODEL),
        'ffn2_w_up': nrm(ks[5], (DEPTH, D_MODEL, D_FF), D_MODEL),
        'ffn2_w_down': nrm(ks[6], (DEPTH, D_FF, D_MODEL), D_FF),
        'norm_pre': 1.0 + 0.02 * jax.random.normal(ks[7], (DEPTH, 3, D_MODEL), f32),
        'norm_post': 1.0 + 0.02 * jax.random.normal(ks[8], (DEPTH, 3, D_MODEL), f32),
        'ab_w_in': nrm(ks[9], (N_EVEN, D_MODEL, AB_IN_WIDTH), D_MODEL),
        'pool_w': nrm(ks[10], (N_EVEN, POOL_GROUPS, POOL_GROUP_DIM, POOL_GROUP_DIM), POOL_GROUP_DIM),
        'pool_scale': 1.0 + 0.02 * jax.random.normal(ks[11], (N_EVEN, POOL_GROUPS, POOL_GROUP_DIM), f32),
        'sgu_norm': 1.0 + 0.02 * jax.random.normal(ks[12], (N_EVEN, SGU_GROUPS, SGU_GROUP_DIM), f32),
        'sgu_w': nrm(ks[13], (N_EVEN, SGU_GROUPS, SGU_CHUNK, SGU_CHUNK), SGU_CHUNK),
        'sgu_b': 1.0 + 0.1 * jax.random.normal(ks[14], (N_EVEN, SGU_GROUPS, SGU_CHUNK), f32),
        'ab_w_out': nrm(ks[15], (N_EVEN, AB_OUT_WIDTH, D_MODEL), AB_OUT_WIDTH),
        'attn_w_qkv': nrm(ks[16], (N_ODD, D_MODEL, 3 * N_HEADS * HEAD_DIM), D_MODEL),
        'attn_w_o': nrm(ks[17], (N_ODD, N_HEADS * HEAD_DIM, D_MODEL), N_HEADS * HEAD_DIM),
    }


def reference(x, ffn1_w_gate, ffn1_w_up, ffn1_w_down, ffn2_w_gate, ffn2_w_up, ffn2_w_down,
              norm_pre, norm_post, ab_w_in, pool_w, pool_scale, sgu_norm, sgu_w, sgu_b, ab_w_out,
              attn_w_qkv, attn_w_o):
    h = x
    for layer in range(DEPTH):
        f1 = swiglu(rms_norm(h, norm_pre[layer, 0]), ffn1_w_gate[layer], ffn1_w_up[layer], ffn1_w_down[layer])
        h = h + 0.5 * rms_norm(f1, norm_post[layer, 0])
        m_in = rms_norm(h, norm_pre[layer, 1])
        i = layer // 2
        if layer % 2 == 0:
            m = pool_sgu_mixer(m_in, ab_w_in[i], pool_w[i], pool_scale[i], sgu_norm[i], sgu_w[i], sgu_b[i], ab_w_out[i])
        else:
            m = moba_mixer(m_in, attn_w_qkv[i], attn_w_o[i])
        h = h + rms_norm(m, norm_post[layer, 1])
        f2 = swiglu(rms_norm(h, norm_pre[layer, 2]), ffn2_w_gate[layer], ffn2_w_up[layer], ffn2_w_down[layer])
        h = h + 0.5 * rms_norm(f2, norm_post[layer, 2])
    return h
```

```python
import functools

import jax
import jax.numpy as jnp
from jax import lax
from jax.experimental import pallas as pl
from jax.experimental.pallas import tpu as pltpu

NORM_EPS = 1e-6
POOL_WINDOWS = (2, 4, 8, 16)
POOL_HALO = 16
SGU_GROUPS = 4
SGU_CHUNK = 128
N_HEADS = 8
ROT_DIM_FRACTION = 4
ROPE_THETA = 500000.0
MOBA_BLOCK = 256
MOBA_TOPK = 3
NEG_INF = -1e30

V7X_VMEM_LIMIT_BYTES = 56 * 1024 * 1024
TOKEN_TILE = 512
KV_GROUP = 4

_BF16 = jnp.bfloat16
_F32 = jnp.float32


def _rms(x, g):
    return x * lax.rsqrt(jnp.mean(x * x, axis=-1, keepdims=True) + NORM_EPS) * g


def _gelu(x):
    return 0.5 * x * (1.0 + lax.erf(x * (0.5 ** 0.5)))


def _dot(a, b):
    return jnp.dot(a, b, preferred_element_type=_F32)


def _dot_nt(a, b, precision=None):
    return lax.dot_general(a, b, (((1,), (1,)), ((), ())), precision=precision,
                           preferred_element_type=_F32)


def _resident(shape):
    zeros = (0,) * len(shape)
    return pl.BlockSpec(shape, lambda *_: zeros, pipeline_mode=pl.Buffered(1))


def _params(*semantics):
    return pltpu.CompilerParams(dimension_semantics=semantics,
                                vmem_limit_bytes=V7X_VMEM_LIMIT_BYTES)


def _ffn_kernel(h_ref, gpre_ref, gpost_ref, wg_ref, wu_ref, wd_ref, o_ref):
    x = h_ref[...]
    xn = _rms(x, gpre_ref[...]).astype(_BF16)
    g = _dot(xn, wg_ref[...])
    u = _dot(xn, wu_ref[...])
    a = (g * jax.nn.sigmoid(g) * u).astype(_BF16)
    f = _dot(a, wd_ref[...])
    o_ref[...] = x + 0.5 * _rms(f, gpost_ref[...])


def _ffn(h, g_pre, g_post, wg, wu, wd):
    n, d = h.shape
    f = wg.shape[1]
    tm = min(TOKEN_TILE, n)
    tok = pl.BlockSpec((tm, d), lambda i: (i, 0))
    return pl.pallas_call(
        _ffn_kernel,
        grid=(n // tm,),
        in_specs=[tok, _resident((1, d)), _resident((1, d)),
                  _resident((d, f)), _resident((d, f)), _resident((f, d))],
        out_specs=tok,
        out_shape=jax.ShapeDtypeStruct((n, d), _F32),
        compiler_params=_params("parallel"),
        name="ffn",
    )(h, g_pre, g_post, wg, wu, wd)


def _pool_sgu_kernel(h_ref, gpre_ref, gpost_ref, win_ref, poolw_ref, pscale_ref, sgun_ref,
                     sguw_ref, sgubt_ref, wout_ref, o_ref, abuf, ybuf):
    t = pl.program_id(1)
    tm = h_ref.shape[1]
    x = h_ref[0]
    z = _dot(_rms(x, gpre_ref[...]).astype(_BF16), win_ref[...])
    gd = poolw_ref.shape[-1]
    pool_w = len(POOL_WINDOWS) * gd
    sgu_w = SGU_GROUPS * gd

    @pl.when(t == 0)
    def _():
        abuf[0:POOL_HALO, :] = jnp.zeros((POOL_HALO, pool_w), _F32)

    abuf[POOL_HALO:POOL_HALO + tm, :] = z[:, :pool_w]
    pos = t * tm + lax.broadcasted_iota(jnp.int32, (tm, 1), 0)
    for g, win in enumerate(POOL_WINDOWS):
        lanes = slice(g * gd, (g + 1) * gd)
        a = abuf[POOL_HALO:POOL_HALO + tm, lanes]
        s = a
        for j in range(1, win):
            s = s + abuf[POOL_HALO - j:POOL_HALO - j + tm, lanes]
        count = jnp.minimum(pos + 1, win).astype(_F32)
        d = (s / count - a).astype(_BF16)
        ybuf[:, lanes] = (_dot(d, poolw_ref[g]) * pscale_ref[g:g + 1, :]).astype(_BF16)
    abuf[0:POOL_HALO, :] = abuf[tm:tm + POOL_HALO, :]

    u = _gelu(z[:, pool_w:pool_w + sgu_w])
    v = _gelu(z[:, pool_w + sgu_w:])
    tr = lax.broadcasted_iota(jnp.int32, (SGU_CHUNK, SGU_CHUNK), 0)
    tc = lax.broadcasted_iota(jnp.int32, (SGU_CHUNK, SGU_CHUNK), 1)
    for g in range(SGU_GROUPS):
        lanes = slice(g * gd, (g + 1) * gd)
        vn = _rms(v[:, lanes], sgun_ref[g:g + 1, :]).astype(_BF16)
        w = jnp.where(tr >= tc, sguw_ref[g], 0.0).astype(_BF16)
        bias = sgubt_ref[:, g:g + 1]
        for c in range(tm // SGU_CHUNK):
            rows = slice(c * SGU_CHUNK, (c + 1) * SGU_CHUNK)
            s = _dot(w, vn[rows]) + bias
            ybuf[rows, pool_w + g * gd:pool_w + (g + 1) * gd] = (u[rows, lanes] * s).astype(_BF16)

    o_ref[0] = x + _rms(_dot(ybuf[...], wout_ref[...]), gpost_ref[...])


def _pool_sgu(h, g_pre, g_post, w_in, pool_w, pool_scale, sgu_norm, sgu_w, sgu_bt, w_out):
    b, s, d = h.shape
    tm = min(TOKEN_TILE, s)
    assert s % tm == 0 and tm % SGU_CHUNK == 0 and tm >= POOL_HALO
    g, gd = pool_scale.shape
    y_w = w_out.shape[0]
    tok = pl.BlockSpec((1, tm, d), lambda i, t: (i, t, 0))
    return pl.pallas_call(
        _pool_sgu_kernel,
        grid=(b, s // tm),
        in_specs=[tok, _resident((1, d)), _resident((1, d)), _resident(w_in.shape),
                  _resident(pool_w.shape), _resident(pool_scale.shape), _resident(sgu_norm.shape),
                  _resident(sgu_w.shape), _resident(sgu_bt.shape), _resident(w_out.shape)],
        out_specs=tok,
        out_shape=jax.ShapeDtypeStruct((b, s, d), _F32),
        scratch_shapes=[pltpu.VMEM((POOL_HALO + tm, g * gd), _F32), pltpu.VMEM((tm, y_w), _BF16)],
        compiler_params=_params("arbitrary", "arbitrary"),
        name="pool_sgu",
    )(h, g_pre, g_post, w_in, pool_w, pool_scale, sgu_norm, sgu_w, sgu_bt, w_out)


def _qkv_kernel(h_ref, gpre_ref, w_ref, cos_ref, sin_lo_ref, sin_hi_ref, q_ref, k_ref, vt_ref, km_ref):
    tm, d = h_ref.shape[1], h_ref.shape[2]
    hd = d // N_HEADS
    half = hd // ROT_DIM_FRACTION // 2
    qkv = _dot(_rms(h_ref[0], gpre_ref[...]).astype(_BF16), w_ref[...])
    cos, sin_lo, sin_hi = cos_ref[...], sin_lo_ref[...], sin_hi_ref[...]

    def rotary(xh):
        return (xh * cos + pltpu.roll(xh, hd - half, 1) * sin_lo + pltpu.roll(xh, half, 1) * sin_hi)

    for h in range(N_HEADS):
        lanes = slice(h * hd, (h + 1) * hd)
        q_ref[0, h] = rotary(qkv[:, h * hd:(h + 1) * hd])
        kr = rotary(qkv[:, d + h * hd:d + (h + 1) * hd])
        k_ref[0, h] = kr.astype(_BF16)
        vt_ref[0, h] = qkv[:, 2 * d + h * hd:2 * d + (h + 1) * hd].T.astype(_BF16)
        for blk in range(tm // MOBA_BLOCK):
            rows = slice(blk * MOBA_BLOCK, (blk + 1) * MOBA_BLOCK)
            km_ref[0, blk, :, lanes] = jnp.mean(kr[rows], axis=0, keepdims=True)


def _qkv(h, g_pre, w_qkv, cos, sin_lo, sin_hi):
    b, s, d = h.shape
    hd = d // N_HEADS
    tm = min(TOKEN_TILE, s)
    assert s % tm == 0 and tm % MOBA_BLOCK == 0
    nb_tile = tm // MOBA_BLOCK
    tab = pl.BlockSpec((tm, hd), lambda i, t: (t, 0))
    return pl.pallas_call(
        _qkv_kernel,
        grid=(b, s // tm),
        in_specs=[pl.BlockSpec((1, tm, d), lambda i, t: (i, t, 0)), _resident((1, d)),
                  _resident(w_qkv.shape), tab, tab, tab],
        out_specs=[pl.BlockSpec((1, N_HEADS, tm, hd), lambda i, t: (i, 0, t, 0)),
                   pl.BlockSpec((1, N_HEADS, tm, hd), lambda i, t: (i, 0, t, 0)),
                   pl.BlockSpec((1, N_HEADS, hd, tm), lambda i, t: (i, 0, 0, t)),
                   pl.BlockSpec((1, nb_tile, 1, d), lambda i, t: (i, t, 0, 0))],
        out_shape=[jax.ShapeDtypeStruct((b, N_HEADS, s, hd), _F32),
                   jax.ShapeDtypeStruct((b, N_HEADS, s, hd), _BF16),
                   jax.ShapeDtypeStruct((b, N_HEADS, hd, s), _BF16),
                   jax.ShapeDtypeStruct((b, s // MOBA_BLOCK, 1, d), _F32)],
        compiler_params=_params("parallel", "parallel"),
        name="qkv_rotary",
    )(h, g_pre, w_qkv, cos, sin_lo, sin_hi)


def _moba_kernel(q_ref, k_ref, vt_ref, km_ref, o_ref, sel_ref):
    i = pl.program_id(2)
    blk = MOBA_BLOCK
    q32 = q_ref[0, 0]
    km = km_ref[0]
    nb, hd = km.shape
    scale = hd ** -0.5

    gate = _dot_nt(km, q32, precision=lax.Precision.HIGHEST)
    bid = lax.broadcasted_iota(jnp.int32, gate.shape, 0)
    past = bid < i
    gate = jnp.where(past, gate, NEG_INF)
    rank = jnp.zeros(gate.shape, jnp.int32)
    for m in range(nb):
        row = gate[m:m + 1, :]
        rank = rank + ((row > gate) | ((row == gate) & (m < bid))).astype(jnp.int32)
    sel_ref[...] = ((rank < min(MOBA_TOPK, nb)) & past).astype(_F32)

    qb = q32.astype(_BF16)

    own = pl.multiple_of(i * blk, blk)
    s = _dot_nt(k_ref[0, 0, pl.ds(own, blk), :], qb) * scale
    key_pos = lax.broadcasted_iota(jnp.int32, s.shape, 0)
    q_pos = lax.broadcasted_iota(jnp.int32, s.shape, 1)
    s = jnp.where(key_pos <= q_pos, s, NEG_INF)
    m0 = jnp.max(s, axis=0, keepdims=True)
    p = jnp.exp(s - m0)
    l0 = jnp.sum(p, axis=0, keepdims=True)
    acc0 = _dot(vt_ref[0, 0, :, pl.ds(own, blk)], p.astype(_BF16))

    def past_group(g, carry):
        m_run, l_run, acc = carry
        start = pl.multiple_of(g * (KV_GROUP * blk), KV_GROUP * blk)
        s = _dot_nt(k_ref[0, 0, pl.ds(start, KV_GROUP * blk), :], qb) * scale
        s = jnp.concatenate(
            [jnp.where(sel_ref[pl.ds(g * KV_GROUP + r, 1), :] > 0.0, s[r * blk:(r + 1) * blk], NEG_INF)
             for r in range(KV_GROUP)], axis=0)
        m_new = jnp.maximum(m_run, jnp.max(s, axis=0, keepdims=True))
        alpha = jnp.exp(m_run - m_new)
        p = jnp.exp(s - m_new)
        l_new = alpha * l_run + jnp.sum(p, axis=0, keepdims=True)
        pv = _dot(vt_ref[0, 0, :, pl.ds(start, KV_GROUP * blk)], p.astype(_BF16))
        return m_new, l_new, alpha * acc + pv

    n_groups = (i + (KV_GROUP - 1)) // KV_GROUP
    _, l_fin, acc = lax.fori_loop(0, n_groups, past_group, (m0, l0, acc0))
    o_ref[0] = (acc / l_fin).T.astype(o_ref.dtype)


def _moba(q, k, vt, kmean):
    b, nh, s, hd = q.shape
    nb = s // MOBA_BLOCK
    assert nb % KV_GROUP == 0
    return pl.pallas_call(
        _moba_kernel,
        grid=(b, nh, nb),
        in_specs=[pl.BlockSpec((1, 1, MOBA_BLOCK, hd), lambda i, h, t: (i, h, t, 0)),
                  pl.BlockSpec((1, 1, s, hd), lambda i, h, t: (i, h, 0, 0)),
                  pl.BlockSpec((1, 1, hd, s), lambda i, h, t: (i, h, 0, 0)),
                  pl.BlockSpec((1, nb, hd), lambda i, h, t: (i, 0, h))],
        out_specs=pl.BlockSpec((1, MOBA_BLOCK, hd), lambda i, h, t: (i, t, h)),
        out_shape=jax.ShapeDtypeStruct((b, s, nh * hd), _BF16),
        scratch_shapes=[pltpu.VMEM((nb, MOBA_BLOCK), _F32)],
        compiler_params=_params("parallel", "parallel", "arbitrary"),
        name="moba_attention",
    )(q, k, vt, kmean)


def _out_proj_kernel(h_ref, o_ref, w_ref, gpost_ref, out_ref):
    out_ref[...] = h_ref[...] + _rms(_dot(o_ref[...], w_ref[...]), gpost_ref[...])


def _out_proj(h, o, w_o, g_post):
    n, d = h.shape
    tm = min(TOKEN_TILE, n)
    tok = pl.BlockSpec((tm, d), lambda i: (i, 0))
    return pl.pallas_call(
        _out_proj_kernel,
        grid=(n // tm,),
        in_specs=[tok, tok, _resident(w_o.shape), _resident((1, d))],
        out_specs=tok,
        out_shape=jax.ShapeDtypeStruct((n, d), _F32),
        compiler_params=_params("parallel"),
        name="attn_out_proj",
    )(h, o, w_o, g_post)


def _rotary_tables(s, hd):
    rot = hd // ROT_DIM_FRACTION
    half = rot // 2
    pos = jnp.arange(s, dtype=_F32)
    inv_freq = 1.0 / (ROPE_THETA ** (jnp.arange(0, rot, 2, dtype=_F32) / rot))
    ang = pos[:, None] * inv_freq[None, :]
    cos, sin = jnp.cos(ang), jnp.sin(ang)
    zeros = jnp.zeros((s, hd - rot), _F32)
    zhalf = jnp.zeros((s, half), _F32)
    return (jnp.concatenate([cos, cos, jnp.ones((s, hd - rot), _F32)], axis=1),
            jnp.concatenate([-sin, zhalf, zeros], axis=1),
            jnp.concatenate([zhalf, sin, zeros], axis=1))


def kernel(x, ffn1_w_gate, ffn1_w_up, ffn1_w_down, ffn2_w_gate, ffn2_w_up, ffn2_w_down, norm_pre, norm_post, ab_w_in, pool_w, pool_scale, sgu_norm, sgu_w, sgu_b, ab_w_out, attn_w_qkv, attn_w_o):
    b, s, d = x.shape
    depth = ffn1_w_gate.shape[0]
    bf = lambda w: w.astype(_BF16)
    tables = _rotary_tables(s, d // N_HEADS)
    h = x.reshape(b * s, d)
    for layer in range(depth):
        pre, post = norm_pre[layer], norm_post[layer]
        h = _ffn(h, pre[0:1], post[0:1], bf(ffn1_w_gate[layer]), bf(ffn1_w_up[layer]), bf(ffn1_w_down[layer]))
        i = layer // 2
        if layer % 2 == 0:
            h = _pool_sgu(h.reshape(b, s, d), pre[1:2], post[1:2], bf(ab_w_in[i]), bf(pool_w[i]), pool_scale[i],
                          sgu_norm[i], sgu_w[i], sgu_b[i].T, bf(ab_w_out[i])).reshape(b * s, d)
        else:
            q, k, vt, kmean = _qkv(h.reshape(b, s, d), pre[1:2], bf(attn_w_qkv[i]), *tables)
            o = _moba(q, k, vt, kmean.reshape(b, s // MOBA_BLOCK, d))
            h = _out_proj(h, o.reshape(b * s, d), bf(attn_w_o[i]), post[1:2])
        h = _ffn(h, pre[2:3], post[2:3], bf(ffn2_w_gate[layer]), bf(ffn2_w_up[layer]), bf(ffn2_w_down[layer]))
    return h.reshape(b, s, d)
```

```python
import functools

import jax
import jax.numpy as jnp
from jax import lax
from jax.experimental import pallas as pl
from jax.experimental.pallas import tpu as pltpu

NORM_EPS = 1e-6
POOL_WINDOWS = (2, 4, 8, 16)
POOL_HALO = 16
SGU_GROUPS = 4
SGU_CHUNK = 128
N_HEADS = 8
ROT_DIM_FRACTION = 4
ROPE_THETA = 500000.0
MOBA_BLOCK = 256
MOBA_TOPK = 3
NEG_INF = -1e30

V7X_VMEM_LIMIT_BYTES = 56 * 1024 * 1024
TOKEN_TILE = 512
LOG2_E = 1.4426950408889634

_BF16 = jnp.bfloat16
_F32 = jnp.float32


def _rms(x, g):
    return x * lax.rsqrt(jnp.mean(x * x, axis=-1, keepdims=True) + NORM_EPS) * g


def _gelu(x):
    return 0.5 * x * (1.0 + lax.erf(x * (0.5 ** 0.5)))


def _dot(a, b):
    return jnp.dot(a, b, preferred_element_type=_F32)


def _dot_nt(a, b, precision=None):
    return lax.dot_general(a, b, (((1,), (1,)), ((), ())), precision=precision,
                           preferred_element_type=_F32)


def _resident(shape):
    zeros = (0,) * len(shape)
    return pl.BlockSpec(shape, lambda *_: zeros, pipeline_mode=pl.Buffered(1))


def _params(*semantics):
    return pltpu.CompilerParams(dimension_semantics=semantics,
                                vmem_limit_bytes=V7X_VMEM_LIMIT_BYTES)


def _ffn_kernel(h_ref, gpre_ref, gpost_ref, wg_ref, wu_ref, wd_ref, o_ref):
    x = h_ref[...]
    xn = _rms(x, gpre_ref[...]).astype(_BF16)
    g = _dot(xn, wg_ref[...])
    u = _dot(xn, wu_ref[...])
    a = (g * jax.nn.sigmoid(g) * u).astype(_BF16)
    f = _dot(a, wd_ref[...])
    o_ref[...] = x + 0.5 * _rms(f, gpost_ref[...])


def _ffn(h, g_pre, g_post, wg, wu, wd):
    n, d = h.shape
    f = wg.shape[1]
    tm = min(TOKEN_TILE, n)
    tok = pl.BlockSpec((tm, d), lambda i: (i, 0))
    return pl.pallas_call(
        _ffn_kernel,
        grid=(n // tm,),
        in_specs=[tok, _resident((1, d)), _resident((1, d)),
                  _resident((d, f)), _resident((d, f)), _resident((f, d))],
        out_specs=tok,
        out_shape=jax.ShapeDtypeStruct((n, d), _F32),
        compiler_params=_params("parallel"),
        name="ffn",
    )(h, g_pre, g_post, wg, wu, wd)


def _pool_sgu_kernel(h_ref, gpre_ref, gpost_ref, win_ref, poolw_ref, pscale_ref, sgun_ref,
                     sguw_ref, sgubt_ref, wout_ref, o_ref, abuf, ybuf):
    t = pl.program_id(1)
    tm = h_ref.shape[1]
    x = h_ref[0]
    z = _dot(_rms(x, gpre_ref[...]).astype(_BF16), win_ref[...])
    gd = poolw_ref.shape[-1]
    pool_w = len(POOL_WINDOWS) * gd
    sgu_w = SGU_GROUPS * gd

    @pl.when(t == 0)
    def _():
        abuf[0:POOL_HALO, :] = jnp.zeros((POOL_HALO, pool_w), _F32)

    abuf[POOL_HALO:POOL_HALO + tm, :] = z[:, :pool_w]
    pos = t * tm + lax.broadcasted_iota(jnp.int32, (tm, 1), 0)
    for g, win in enumerate(POOL_WINDOWS):
        lanes = slice(g * gd, (g + 1) * gd)
        a = abuf[POOL_HALO:POOL_HALO + tm, lanes]
        s = a
        for j in range(1, win):
            s = s + abuf[POOL_HALO - j:POOL_HALO - j + tm, lanes]
        count = jnp.minimum(pos + 1, win).astype(_F32)
        d = (s / count - a).astype(_BF16)
        ybuf[:, lanes] = (_dot(d, poolw_ref[g]) * pscale_ref[g:g + 1, :]).astype(_BF16)
    abuf[0:POOL_HALO, :] = abuf[tm:tm + POOL_HALO, :]

    u = _gelu(z[:, pool_w:pool_w + sgu_w])
    v = _gelu(z[:, pool_w + sgu_w:])
    tr = lax.broadcasted_iota(jnp.int32, (SGU_CHUNK, SGU_CHUNK), 0)
    tc = lax.broadcasted_iota(jnp.int32, (SGU_CHUNK, SGU_CHUNK), 1)
    for g in range(SGU_GROUPS):
        lanes = slice(g * gd, (g + 1) * gd)
        vn = _rms(v[:, lanes], sgun_ref[g:g + 1, :]).astype(_BF16)
        w = jnp.where(tr >= tc, sguw_ref[g], 0.0).astype(_BF16)
        bias = sgubt_ref[:, g:g + 1]
        for c in range(tm // SGU_CHUNK):
            rows = slice(c * SGU_CHUNK, (c + 1) * SGU_CHUNK)
            s = _dot(w, vn[rows]) + bias
            ybuf[rows, pool_w + g * gd:pool_w + (g + 1) * gd] = (u[rows, lanes] * s).astype(_BF16)

    o_ref[0] = x + _rms(_dot(ybuf[...], wout_ref[...]), gpost_ref[...])


def _pool_sgu(h, g_pre, g_post, w_in, pool_w, pool_scale, sgu_norm, sgu_w, sgu_bt, w_out):
    b, s, d = h.shape
    tm = min(TOKEN_TILE, s)
    assert s % tm == 0 and tm % SGU_CHUNK == 0 and tm >= POOL_HALO
    g, gd = pool_scale.shape
    y_w = w_out.shape[0]
    tok = pl.BlockSpec((1, tm, d), lambda i, t: (i, t, 0))
    return pl.pallas_call(
        _pool_sgu_kernel,
        grid=(b, s // tm),
        in_specs=[tok, _resident((1, d)), _resident((1, d)), _resident(w_in.shape),
                  _resident(pool_w.shape), _resident(pool_scale.shape), _resident(sgu_norm.shape),
                  _resident(sgu_w.shape), _resident(sgu_bt.shape), _resident(w_out.shape)],
        out_specs=tok,
        out_shape=jax.ShapeDtypeStruct((b, s, d), _F32),
        scratch_shapes=[pltpu.VMEM((POOL_HALO + tm, g * gd), _F32), pltpu.VMEM((tm, y_w), _BF16)],
        compiler_params=_params("arbitrary", "arbitrary"),
        name="pool_sgu",
    )(h, g_pre, g_post, w_in, pool_w, pool_scale, sgu_norm, sgu_w, sgu_bt, w_out)


def _qkv_kernel(h_ref, gpre_ref, w_ref, cos_ref, sin_lo_ref, sin_hi_ref, q_ref, k_ref, vt_ref, km_ref):
    tm, d = h_ref.shape[1], h_ref.shape[2]
    hd = d // N_HEADS
    half = hd // ROT_DIM_FRACTION // 2
    qkv = _dot(_rms(h_ref[0], gpre_ref[...]).astype(_BF16), w_ref[...])
    cos, sin_lo, sin_hi = cos_ref[...], sin_lo_ref[...], sin_hi_ref[...]

    def rotary(xh):
        return (xh * cos + pltpu.roll(xh, hd - half, 1) * sin_lo + pltpu.roll(xh, half, 1) * sin_hi)

    for h in range(N_HEADS):
        lanes = slice(h * hd, (h + 1) * hd)
        q_ref[0, h] = rotary(qkv[:, h * hd:(h + 1) * hd])
        kr = rotary(qkv[:, d + h * hd:d + (h + 1) * hd])
        k_ref[0, h] = kr.astype(_BF16)
        vt_ref[0, h] = qkv[:, 2 * d + h * hd:2 * d + (h + 1) * hd].T.astype(_BF16)
        for blk in range(tm // MOBA_BLOCK):
            rows = slice(blk * MOBA_BLOCK, (blk + 1) * MOBA_BLOCK)
            km_ref[0, blk, :, lanes] = jnp.mean(kr[rows], axis=0, keepdims=True)


def _qkv(h, g_pre, w_qkv, cos, sin_lo, sin_hi):
    b, s, d = h.shape
    hd = d // N_HEADS
    tm = min(TOKEN_TILE, s)
    assert s % tm == 0 and tm % MOBA_BLOCK == 0
    nb_tile = tm // MOBA_BLOCK
    tab = pl.BlockSpec((tm, hd), lambda i, t: (t, 0))
    return pl.pallas_call(
        _qkv_kernel,
        grid=(b, s // tm),
        in_specs=[pl.BlockSpec((1, tm, d), lambda i, t: (i, t, 0)), _resident((1, d)),
                  _resident(w_qkv.shape), tab, tab, tab],
        out_specs=[pl.BlockSpec((1, N_HEADS, tm, hd), lambda i, t: (i, 0, t, 0)),
                   pl.BlockSpec((1, N_HEADS, tm, hd), lambda i, t: (i, 0, t, 0)),
                   pl.BlockSpec((1, N_HEADS, hd, tm), lambda i, t: (i, 0, 0, t)),
                   pl.BlockSpec((1, nb_tile, 1, d), lambda i, t: (i, t, 0, 0))],
        out_shape=[jax.ShapeDtypeStruct((b, N_HEADS, s, hd), _F32),
                   jax.ShapeDtypeStruct((b, N_HEADS, s, hd), _BF16),
                   jax.ShapeDtypeStruct((b, N_HEADS, hd, s), _BF16),
                   jax.ShapeDtypeStruct((b, s // MOBA_BLOCK, 1, d), _F32)],
        compiler_params=_params("parallel", "parallel"),
        name="qkv_rotary",
    )(h, g_pre, w_qkv, cos, sin_lo, sin_hi)


def _interleave(first, second):
    n, m = len(first), len(second)
    order = sorted([((i + 0.5) / n, 0, i) for i in range(n)] + [((i + 0.5) / m, 1, i) for i in range(m)])
    return [(first, second)[which][i] for _, which, i in order]


def _moba_kernel(q_ref, k_ref, vt_ref, km_ref, o_ref, bias_ref, t_ref):
    blk = MOBA_BLOCK
    sub = 8
    km = km_ref[0]
    nb, hd = km.shape
    to_log2 = (hd ** -0.5) * LOG2_E
    causal = (lax.broadcasted_iota(jnp.int32, (blk, blk), 0) <= lax.broadcasted_iota(jnp.int32, (blk, blk), 1))
    state = [dict() for _ in range(nb)]

    def rows(j):
        return slice(j * blk, (j + 1) * blk)

    def gate_step(c):
        q32 = q_ref[0, 0, rows(c), :]
        gate = _dot_nt(km, q32, precision=lax.Precision.HIGHEST)
        bid = lax.broadcasted_iota(jnp.int32, gate.shape, 0)
        past = bid < c
        gate = jnp.where(past, gate, NEG_INF)
        rank = jnp.zeros(gate.shape, jnp.int32)
        for m in range(nb):
            row = gate[m:m + 1, :]
            rank = rank + ((row > gate) | ((row == gate) & (m < bid))).astype(jnp.int32)
        bias_ref[:, rows(c)] = jnp.where((rank < min(MOBA_TOPK, nb)) & past, 0.0, NEG_INF)
        state[c].update(qb=q32.astype(_BF16), m=None, l=None, acc=None)

    def score_step(c, j):
        st = state[c]
        s = _dot_nt(k_ref[0, 0, rows(j), :], st["qb"]) * to_log2
        if j < c:
            s = s + bias_ref[j:j + 1, rows(c)]
        else:
            s = jnp.where(causal, s, NEG_INF)
        t_ref[c % 2, rows(j), :] = s
        part = jnp.max(s.reshape(blk // sub, sub, blk), axis=0)
        st["m"] = part if st["m"] is None else jnp.maximum(st["m"], part)

    def softmax_step(c, j):
        st = state[c]
        if st["l"] is None:
            st["m"] = jnp.max(st["m"], axis=0, keepdims=True)
        p = jnp.exp2(t_ref[c % 2, rows(j), :] - st["m"])
        part = jnp.sum(p.reshape(blk // sub, sub, blk), axis=0)
        pv = _dot(vt_ref[0, 0, :, rows(j)], p.astype(_BF16))
        st["l"] = part if st["l"] is None else st["l"] + part
        st["acc"] = pv if st["acc"] is None else st["acc"] + pv

    def finish(c):
        st = state[c]
        inv_l = 1.0 / jnp.sum(st["l"], axis=0, keepdims=True)
        o_ref[0, rows(c), :] = (st["acc"] * inv_l).T.astype(o_ref.dtype)
        st.clear()

    def steps(fn, c):
        return [functools.partial(fn, c, j) for j in range(c + 1)]

    gate_step(0)
    score_step(0, 0)
    for c in range(nb):
        ahead = [functools.partial(gate_step, c + 1)] + steps(score_step, c + 1) if c + 1 < nb else []
        for step in _interleave(ahead, steps(softmax_step, c)) if ahead else steps(softmax_step, c):
            step()
        finish(c)


def _moba(q, k, vt, kmean):
    b, nh, s, hd = q.shape
    nb = s // MOBA_BLOCK
    return pl.pallas_call(
        _moba_kernel,
        grid=(b, nh),
        in_specs=[pl.BlockSpec((1, 1, s, hd), lambda i, h: (i, h, 0, 0)),
                  pl.BlockSpec((1, 1, s, hd), lambda i, h: (i, h, 0, 0)),
                  pl.BlockSpec((1, 1, hd, s), lambda i, h: (i, h, 0, 0)),
                  pl.BlockSpec((1, nb, hd), lambda i, h: (i, 0, h))],
        out_specs=pl.BlockSpec((1, s, hd), lambda i, h: (i, 0, h)),
        out_shape=jax.ShapeDtypeStruct((b, s, nh * hd), _BF16),
        scratch_shapes=[pltpu.VMEM((nb, s), _F32), pltpu.VMEM((2, s, MOBA_BLOCK), _F32)],
        compiler_params=_params("parallel", "parallel"),
        name="moba_attention",
    )(q, k, vt, kmean)


def _out_proj_kernel(h_ref, o_ref, w_ref, gpost_ref, out_ref):
    out_ref[...] = h_ref[...] + _rms(_dot(o_ref[...], w_ref[...]), gpost_ref[...])


def _out_proj(h, o, w_o, g_post):
    n, d = h.shape
    tm = min(TOKEN_TILE, n)
    tok = pl.BlockSpec((tm, d), lambda i: (i, 0))
    return pl.pallas_call(
        _out_proj_kernel,
        grid=(n // tm,),
        in_specs=[tok, tok, _resident(w_o.shape), _resident((1, d))],
        out_specs=tok,
        out_shape=jax.ShapeDtypeStruct((n, d), _F32),
        compiler_params=_params("parallel"),
        name="attn_out_proj",
    )(h, o, w_o, g_post)


def _rotary_tables(s, hd):
    rot = hd // ROT_DIM_FRACTION
    half = rot // 2
    pos = jnp.arange(s, dtype=_F32)
    inv_freq = 1.0 / (ROPE_THETA ** (jnp.arange(0, rot, 2, dtype=_F32) / rot))
    ang = pos[:, None] * inv_freq[None, :]
    cos, sin = jnp.cos(ang), jnp.sin(ang)
    zeros = jnp.zeros((s, hd - rot), _F32)
    zhalf = jnp.zeros((s, half), _F32)
    return (jnp.concatenate([cos, cos, jnp.ones((s, hd - rot), _F32)], axis=1),
            jnp.concatenate([-sin, zhalf, zeros], axis=1),
            jnp.concatenate([zhalf, sin, zeros], axis=1))


def kernel(x, ffn1_w_gate, ffn1_w_up, ffn1_w_down, ffn2_w_gate, ffn2_w_up, ffn2_w_down, norm_pre, norm_post, ab_w_in, pool_w, pool_scale, sgu_norm, sgu_w, sgu_b, ab_w_out, attn_w_qkv, attn_w_o):
    b, s, d = x.shape
    depth = ffn1_w_gate.shape[0]
    bf = lambda w: w.astype(_BF16)
    tables = _rotary_tables(s, d // N_HEADS)
    h = x.reshape(b * s, d)
    for layer in range(depth):
        pre, post = norm_pre[layer], norm_post[layer]
        h = _ffn(h, pre[0:1], post[0:1], bf(ffn1_w_gate[layer]), bf(ffn1_w_up[layer]), bf(ffn1_w_down[layer]))
        i = layer // 2
        if layer % 2 == 0:
            h = _pool_sgu(h.reshape(b, s, d), pre[1:2], post[1:2], bf(ab_w_in[i]), bf(pool_w[i]), pool_scale[i],
                          sgu_norm[i], sgu_w[i], sgu_b[i].T, bf(ab_w_out[i])).reshape(b * s, d)
        else:
            q, k, vt, kmean = _qkv(h.reshape(b, s, d), pre[1:2], bf(attn_w_qkv[i]), *tables)
            o = _moba(q, k, vt, kmean.reshape(b, s // MOBA_BLOCK, d))
            h = _out_proj(h, o.reshape(b * s, d), bf(attn_w_o[i]), post[1:2])
        h = _ffn(h, pre[2:3], post[2:3], bf(ffn2_w_gate[layer]), bf(ffn2_w_up[layer]), bf(ffn2_w_down[layer]))
    return h.reshape(b, s, d)
```

```python
import functools

import jax
import jax.numpy as jnp
from jax import lax
from jax.experimental import pallas as pl
from jax.experimental.pallas import tpu as pltpu

NORM_EPS = 1e-6
POOL_WINDOWS = (2, 4, 8, 16)
POOL_HALO = 16
SGU_GROUPS = 4
SGU_CHUNK = 128
N_HEADS = 8
ROT_DIM_FRACTION = 4
ROPE_THETA = 500000.0
MOBA_BLOCK = 256
MOBA_TOPK = 3
NEG_INF = -1e30
LOG2_E = 1.4426950408889634

V7X_VMEM_LIMIT_BYTES = 56 * 1024 * 1024
F32_SUBLANES = 8
BF16_SUBLANES = 16
TOKEN_TILE = 1024
SUBTILE = 256
QKV_TILE = 512
KV_GROUP = 4
SOFTMAX_LAG = 8
GATE_LEAD = 10
SCORE_BUFFERS = 4

_BF16 = jnp.bfloat16
_F32 = jnp.float32


def _rms(x, g):
    return x * lax.rsqrt(jnp.mean(x * x, axis=-1, keepdims=True) + NORM_EPS) * g


def _gelu(x):
    return 0.5 * x * (1.0 + lax.erf(x * (0.5 ** 0.5)))


def _dot(a, b):
    return jnp.dot(a, b, preferred_element_type=_F32)


def _dot_nt(a, b, precision=None):
    return lax.dot_general(a, b, (((1,), (1,)), ((), ())), precision=precision,
                           preferred_element_type=_F32)


def _resident(shape):
    zeros = (0,) * len(shape)
    return pl.BlockSpec(shape, lambda *_: zeros, pipeline_mode=pl.Buffered(1))


def _params(*semantics):
    return pltpu.CompilerParams(dimension_semantics=semantics,
                                vmem_limit_bytes=V7X_VMEM_LIMIT_BYTES)


def _subtile_rows(tm):
    ts = min(SUBTILE, tm)
    assert tm % ts == 0
    return [slice(r * ts, (r + 1) * ts) for r in range(tm // ts)]


def _ffn_kernel(*refs, project_attention):
    if project_attention:
        h_ref, gpre_ref, gpost_ref, wg_ref, wu_ref, wd_ref, attn_ref, wo_ref, gmix_ref, o_ref = refs
    else:
        h_ref, gpre_ref, gpost_ref, wg_ref, wu_ref, wd_ref, o_ref = refs
    tiles = _subtile_rows(h_ref.shape[0])
    live = [dict() for _ in tiles]

    def load(r):
        x = h_ref[tiles[r], :]
        if project_attention:
            x = x + _rms(_dot(attn_ref[tiles[r], :], wo_ref[...]), gmix_ref[...])
            o_ref[tiles[r], :] = x
        live[r]["xn"] = _rms(x, gpre_ref[...]).astype(_BF16)

    def expand(r):
        xn = live[r].pop("xn")
        g = _dot(xn, wg_ref[...])
        u = _dot(xn, wu_ref[...])
        live[r]["a"] = (g * jax.nn.sigmoid(g) * u).astype(_BF16)

    def contract(r):
        live[r]["f"] = _dot(live[r].pop("a"), wd_ref[...])

    def store(r):
        x = (o_ref if project_attention else h_ref)[tiles[r], :]
        o_ref[tiles[r], :] = x + 0.5 * _rms(live[r].pop("f"), gpost_ref[...])

    load(0)
    expand(0)
    for r in range(len(tiles)):
        if r + 1 < len(tiles):
            load(r + 1)
        contract(r)
        if r + 1 < len(tiles):
            expand(r + 1)
        store(r)


def _ffn(h, g_pre, g_post, wg, wu, wd, attention=None):
    n, d = h.shape
    f = wg.shape[1]
    tm = min(TOKEN_TILE, n)
    tok = pl.BlockSpec((tm, d), lambda i: (i, 0))
    in_specs = [tok, _resident((1, d)), _resident((1, d)),
                _resident((d, f)), _resident((d, f)), _resident((f, d))]
    args = [h, g_pre, g_post, wg, wu, wd]
    if attention is not None:
        in_specs += [tok, _resident(attention[1].shape), _resident((1, d))]
        args += list(attention)
    return pl.pallas_call(
        functools.partial(_ffn_kernel, project_attention=attention is not None),
        grid=(n // tm,),
        in_specs=in_specs,
        out_specs=tok,
        out_shape=jax.ShapeDtypeStruct((n, d), _F32),
        compiler_params=_params("parallel"),
        name="ffn_attn_proj" if attention is not None else "ffn",
    )(*args)


def _pool_sgu_kernel(h_ref, gpre_ref, gpost_ref, win_ref, poolw_ref, pscale_ref, sgun_ref,
                     sguw_ref, sgubt_ref, wout_ref, o_ref, abuf, ybuf):
    t = pl.program_id(1)
    tm = h_ref.shape[1]
    tiles = _subtile_rows(tm)
    ts = tm // len(tiles)
    gd = poolw_ref.shape[-1]
    pool_w = len(POOL_WINDOWS) * gd
    sgu_w = SGU_GROUPS * gd
    live = [dict() for _ in tiles]
    causal = (lax.broadcasted_iota(jnp.int32, (SGU_CHUNK, SGU_CHUNK), 0)
              >= lax.broadcasted_iota(jnp.int32, (SGU_CHUNK, SGU_CHUNK), 1))

    @pl.when(t == 0)
    def _():
        abuf[0:POOL_HALO, :] = jnp.zeros((POOL_HALO, pool_w), _F32)

    def project_in(r):
        z = _dot(_rms(h_ref[0, tiles[r], :], gpre_ref[...]).astype(_BF16), win_ref[...])
        abuf[POOL_HALO + r * ts:POOL_HALO + (r + 1) * ts, :] = z[:, :pool_w]
        live[r].update(u=z[:, pool_w:pool_w + sgu_w], v=z[:, pool_w + sgu_w:])

    def mix(r):
        base = POOL_HALO + r * ts
        pos = t * tm + r * ts + lax.broadcasted_iota(jnp.int32, (ts, 1), 0)
        for g, win in enumerate(POOL_WINDOWS):
            lanes = slice(g * gd, (g + 1) * gd)
            a = abuf[base:base + ts, lanes]
            s = a
            for j in range(1, win):
                s = s + abuf[base - j:base - j + ts, lanes]
            count = jnp.minimum(pos + 1, win).astype(_F32)
            d = (s / count - a).astype(_BF16)
            ybuf[tiles[r], lanes] = (_dot(d, poolw_ref[g]) * pscale_ref[g:g + 1, :]).astype(_BF16)
        u = _gelu(live[r].pop("u"))
        v = _gelu(live[r].pop("v"))
        for g in range(SGU_GROUPS):
            lanes = slice(g * gd, (g + 1) * gd)
            vn = _rms(v[:, lanes], sgun_ref[g:g + 1, :]).astype(_BF16)
            w = jnp.where(causal, sguw_ref[g], 0.0).astype(_BF16)
            bias = sgubt_ref[:, g:g + 1]
            for c in range(ts // SGU_CHUNK):
                rows = slice(c * SGU_CHUNK, (c + 1) * SGU_CHUNK)
                s = _dot(w, vn[rows]) + bias
                ybuf[r * ts + c * SGU_CHUNK:r * ts + (c + 1) * SGU_CHUNK,
                     pool_w + g * gd:pool_w + (g + 1) * gd] = (u[rows, lanes] * s).astype(_BF16)

    def project_out(r):
        m = _dot(ybuf[tiles[r], :], wout_ref[...])
        o_ref[0, tiles[r], :] = h_ref[0, tiles[r], :] + _rms(m, gpost_ref[...])

    project_in(0)
    for r in range(len(tiles)):
        if r + 1 < len(tiles):
            project_in(r + 1)
        mix(r)
        project_out(r)
    abuf[0:POOL_HALO, :] = abuf[tm:tm + POOL_HALO, :]


def _pool_sgu(h, g_pre, g_post, w_in, pool_w, pool_scale, sgu_norm, sgu_w, sgu_bt, w_out):
    b, s, d = h.shape
    tm = min(TOKEN_TILE, s)
    assert s % tm == 0 and min(SUBTILE, tm) % SGU_CHUNK == 0 and min(SUBTILE, tm) >= POOL_HALO
    g, gd = pool_scale.shape
    y_w = w_out.shape[0]
    tok = pl.BlockSpec((1, tm, d), lambda i, t: (i, t, 0))
    return pl.pallas_call(
        _pool_sgu_kernel,
        grid=(b, s // tm),
        in_specs=[tok, _resident((1, d)), _resident((1, d)), _resident(w_in.shape),
                  _resident(pool_w.shape), _resident(pool_scale.shape), _resident(sgu_norm.shape),
                  _resident(sgu_w.shape), _resident(sgu_bt.shape), _resident(w_out.shape)],
        out_specs=tok,
        out_shape=jax.ShapeDtypeStruct((b, s, d), _F32),
        scratch_shapes=[pltpu.VMEM((POOL_HALO + tm, g * gd), _F32), pltpu.VMEM((tm, y_w), _BF16)],
        compiler_params=_params("arbitrary", "arbitrary"),
        name="pool_sgu",
    )(h, g_pre, g_post, w_in, pool_w, pool_scale, sgu_norm, sgu_w, sgu_bt, w_out)


def _qkv_kernel(h_ref, gpre_ref, w_ref, cos_ref, sin_lo_ref, sin_hi_ref, q_ref, k_ref, vt_ref, km_ref):
    tm, d = h_ref.shape[1], h_ref.shape[2]
    hd = d // N_HEADS
    half = hd // ROT_DIM_FRACTION // 2
    qkv = _dot(_rms(h_ref[0], gpre_ref[...]).astype(_BF16), w_ref[...])
    cos, sin_lo, sin_hi = cos_ref[...], sin_lo_ref[...], sin_hi_ref[...]

    def rotary(xh):
        return (xh * cos + pltpu.roll(xh, hd - half, 1) * sin_lo + pltpu.roll(xh, half, 1) * sin_hi)

    key_block = (pl.program_id(1) * tm + lax.broadcasted_iota(jnp.int32, (tm, hd), 0)) // MOBA_BLOCK
    block_onehot = (key_block == lax.broadcasted_iota(jnp.int32, (tm, hd), 1)).astype(_BF16)
    ones_rows = jnp.ones((vt_ref.shape[2] - hd, tm), _BF16)

    for h in range(N_HEADS):
        lanes = slice(h * hd, (h + 1) * hd)
        q_ref[0, h] = rotary(qkv[:, h * hd:(h + 1) * hd])
        kr = rotary(qkv[:, d + h * hd:d + (h + 1) * hd])
        k_ref[0, h, :, 0:hd] = kr.astype(_BF16)
        k_ref[0, h, :, hd:2 * hd] = block_onehot
        vt_ref[0, h, 0:hd, :] = qkv[:, 2 * d + h * hd:2 * d + (h + 1) * hd].T.astype(_BF16)
        vt_ref[0, h, hd:, :] = ones_rows
        for blk in range(tm // MOBA_BLOCK):
            rows = slice(blk * MOBA_BLOCK, (blk + 1) * MOBA_BLOCK)
            km_ref[0, blk, :, lanes] = jnp.mean(kr[rows], axis=0, keepdims=True)


def _qkv(h, g_pre, w_qkv, cos, sin_lo, sin_hi):
    b, s, d = h.shape
    hd = d // N_HEADS
    tm = min(QKV_TILE, s)
    assert s % tm == 0 and tm % MOBA_BLOCK == 0 and s // MOBA_BLOCK <= hd
    nb_tile = tm // MOBA_BLOCK
    tab = pl.BlockSpec((tm, hd), lambda i, t: (t, 0))
    return pl.pallas_call(
        _qkv_kernel,
        grid=(b, s // tm),
        in_specs=[pl.BlockSpec((1, tm, d), lambda i, t: (i, t, 0)), _resident((1, d)),
                  _resident(w_qkv.shape), tab, tab, tab],
        out_specs=[pl.BlockSpec((1, N_HEADS, tm, hd), lambda i, t: (i, 0, t, 0)),
                   pl.BlockSpec((1, N_HEADS, tm, 2 * hd), lambda i, t: (i, 0, t, 0)),
                   pl.BlockSpec((1, N_HEADS, hd + BF16_SUBLANES, tm), lambda i, t: (i, 0, 0, t)),
                   pl.BlockSpec((1, nb_tile, 1, d), lambda i, t: (i, t, 0, 0))],
        out_shape=[jax.ShapeDtypeStruct((b, N_HEADS, s, hd), _F32),
                   jax.ShapeDtypeStruct((b, N_HEADS, s, 2 * hd), _BF16),
                   jax.ShapeDtypeStruct((b, N_HEADS, hd + BF16_SUBLANES, s), _BF16),
                   jax.ShapeDtypeStruct((b, s // MOBA_BLOCK, 1, d), _F32)],
        compiler_params=_params("parallel", "parallel"),
        name="qkv_rotary",
    )(h, g_pre, w_qkv, cos, sin_lo, sin_hi)


def _moba_kernel(q_ref, k_ref, vt_ref, km_ref, o_ref, *t_refs):
    blk = MOBA_BLOCK
    sub = F32_SUBLANES
    km = km_ref[0]
    nb, hd = km.shape
    to_log2 = (hd ** -0.5) * LOG2_E
    causal = (lax.broadcasted_iota(jnp.int32, (blk, blk), 0) <= lax.broadcasted_iota(jnp.int32, (blk, blk), 1))
    state = [dict() for _ in range(nb)]

    def rows(j, n=1):
        return slice(j * blk, (j + n) * blk)

    def groups(c):
        return [(j, min(KV_GROUP, c + 1 - j)) for j in range(0, c + 1, KV_GROUP)]

    def gate_step(c):
        q32 = q_ref[0, 0, rows(c), :]
        gate = _dot_nt(km, q32, precision=lax.Precision.HIGHEST)
        bid = lax.broadcasted_iota(jnp.int32, gate.shape, 0)
        past = bid < c
        gate = jnp.where(past, gate, NEG_INF)
        rank = jnp.zeros(gate.shape, jnp.int32)
        for m in range(nb):
            row = gate[m:m + 1, :]
            rank = rank + ((row > gate) | ((row == gate) & (m < bid))).astype(jnp.int32)
        bias = jnp.where(past & (rank >= min(MOBA_TOPK, nb)), NEG_INF, 0.0)
        bias = jnp.concatenate([bias, jnp.zeros((hd - nb, blk), _F32)], axis=0).T
        qa = jnp.concatenate([(q32 * to_log2).astype(_BF16), bias.astype(_BF16)], axis=1)
        state[c].update(qa=qa, m=None, acc=None)

    def score_step(c, j, n):
        st = state[c]
        s = _dot_nt(k_ref[0, 0, rows(j, n), :], st["qa"])
        for r in range(n):
            sr = s[r * blk:(r + 1) * blk]
            if j + r == c:
                sr = jnp.where(causal, sr, NEG_INF)
            t_refs[c % SCORE_BUFFERS][rows(j + r), :] = sr
            part = jnp.max(sr.reshape(blk // sub, sub, blk), axis=0)
            st["m"] = part if st["m"] is None else jnp.maximum(st["m"], part)

    def softmax_step(c, j, n):
        st = state[c]
        if st["acc"] is None:
            st["m"] = jnp.max(st["m"], axis=0, keepdims=True)
        p = jnp.exp2((t_refs[c % SCORE_BUFFERS][rows(j, n), :] - st["m"]).astype(_BF16))
        pv = _dot(vt_ref[0, 0, :, rows(j, n)], p)
        st["acc"] = pv if st["acc"] is None else st["acc"] + pv

    def finish(c):
        acc = state[c]["acc"]
        inv_l = 1.0 / acc[hd:hd + 1, :]
        o_ref[0, rows(c), :] = (acc[0:hd] * inv_l).T.astype(o_ref.dtype)
        state[c].clear()

    done_before = lambda c: c * (c + 1) // 2
    schedule = []
    for c in range(nb):
        schedule.append((done_before(c) - GATE_LEAD, 0, ("gate", c), functools.partial(gate_step, c)))
        for j, n in groups(c):
            schedule.append((done_before(c) + j, 1, ("score", c), functools.partial(score_step, c, j, n)))
            schedule.append((done_before(c + 1) + SOFTMAX_LAG + j, 2, ("softmax", c),
                             functools.partial(softmax_step, c, j, n)))
        schedule.append((done_before(c + 1) + SOFTMAX_LAG + c + 0.5, 3, ("finish", c), functools.partial(finish, c)))
    schedule.sort(key=lambda e: e[:2])
    tags = [e[2] for e in schedule]
    for c in range(nb - SCORE_BUFFERS):
        last_read = max(i for i, tag in enumerate(tags) if tag == ("softmax", c))
        assert last_read < tags.index(("score", c + SCORE_BUFFERS))
    for *_, step in schedule:
        step()


def _moba(q, k, vt, kmean):
    b, nh, s, hd = q.shape
    nb = s // MOBA_BLOCK
    return pl.pallas_call(
        _moba_kernel,
        grid=(b, nh),
        in_specs=[pl.BlockSpec((1, 1, s, hd), lambda i, h: (i, h, 0, 0)),
                  pl.BlockSpec((1, 1, s, k.shape[3]), lambda i, h: (i, h, 0, 0)),
                  pl.BlockSpec((1, 1, vt.shape[2], s), lambda i, h: (i, h, 0, 0)),
                  pl.BlockSpec((1, nb, hd), lambda i, h: (i, 0, h))],
        out_specs=pl.BlockSpec((1, s, hd), lambda i, h: (i, 0, h)),
        out_shape=jax.ShapeDtypeStruct((b, s, nh * hd), _BF16),
        scratch_shapes=[pltpu.VMEM((s, MOBA_BLOCK), _F32)] * SCORE_BUFFERS,
        compiler_params=_params("parallel", "parallel"),
        name="moba_attention",
    )(q, k, vt, kmean)


def _rotary_tables(s, hd):
    rot = hd // ROT_DIM_FRACTION
    half = rot // 2
    pos = jnp.arange(s, dtype=_F32)
    inv_freq = 1.0 / (ROPE_THETA ** (jnp.arange(0, rot, 2, dtype=_F32) / rot))
    ang = pos[:, None] * inv_freq[None, :]
    cos, sin = jnp.cos(ang), jnp.sin(ang)
    zeros = jnp.zeros((s, hd - rot), _F32)
    zhalf = jnp.zeros((s, half), _F32)
    return (jnp.concatenate([cos, cos, jnp.ones((s, hd - rot), _F32)], axis=1),
            jnp.concatenate([-sin, zhalf, zeros], axis=1),
            jnp.concatenate([zhalf, sin, zeros], axis=1))


def kernel(x, ffn1_w_gate, ffn1_w_up, ffn1_w_down, ffn2_w_gate, ffn2_w_up, ffn2_w_down, norm_pre, norm_post, ab_w_in, pool_w, pool_scale, sgu_norm, sgu_w, sgu_b, ab_w_out, attn_w_qkv, attn_w_o):
    b, s, d = x.shape
    depth = ffn1_w_gate.shape[0]
    bf = lambda w: w.astype(_BF16)
    tables = _rotary_tables(s, d // N_HEADS)
    h = x.reshape(b * s, d)
    for layer in range(depth):
        pre, post = norm_pre[layer], norm_post[layer]
        h = _ffn(h, pre[0:1], post[0:1], bf(ffn1_w_gate[layer]), bf(ffn1_w_up[layer]), bf(ffn1_w_down[layer]))
        i = layer // 2
        attention = None
        if layer % 2 == 0:
            h = _pool_sgu(h.reshape(b, s, d), pre[1:2], post[1:2], bf(ab_w_in[i]), bf(pool_w[i]), pool_scale[i],
                          sgu_norm[i], sgu_w[i], sgu_b[i].T, bf(ab_w_out[i])).reshape(b * s, d)
        else:
            q, k, vt, kmean = _qkv(h.reshape(b, s, d), pre[1:2], bf(attn_w_qkv[i]), *tables)
            o = _moba(q, k, vt, kmean.reshape(b, s // MOBA_BLOCK, d))
            attention = (o.reshape(b * s, d), bf(attn_w_o[i]), post[1:2])
        h = _ffn(h, pre[2:3], post[2:3], bf(ffn2_w_gate[layer]), bf(ffn2_w_up[layer]), bf(ffn2_w_down[layer]),
                 attention=attention)
    return h.reshape(b, s, d)
```

```python
import functools

import jax
import jax.numpy as jnp
from jax import lax
from jax.experimental import pallas as pl
from jax.experimental.pallas import tpu as pltpu

NORM_EPS = 1e-6
POOL_WINDOWS = (2, 4, 8, 16)
POOL_HALO = 16
SGU_GROUPS = 4
SGU_CHUNK = 128
N_HEADS = 8
ROT_DIM_FRACTION = 4
ROPE_THETA = 500000.0
MOBA_BLOCK = 256
MOBA_TOPK = 3
NEG_INF = -1e30
LOG2_E = 1.4426950408889634

V7X_VMEM_LIMIT_BYTES = 56 * 1024 * 1024
F32_SUBLANES = 8
BF16_SUBLANES = 16
TOKEN_TILE = 1024
SUBTILE = 256
QKV_TILE = 512
KV_GROUP = 4
SOFTMAX_LAG = 8
GATE_LEAD = 10
SCORE_BUFFERS = 4

_BF16 = jnp.bfloat16
_F32 = jnp.float32


def _rms(x, g):
    return x * lax.rsqrt(jnp.mean(x * x, axis=-1, keepdims=True) + NORM_EPS) * g


def _gelu(x):
    return 0.5 * x * (1.0 + lax.erf(x * (0.5 ** 0.5)))


def _dot(a, b):
    return jnp.dot(a, b, preferred_element_type=_F32)


def _dot_nt(a, b, precision=None):
    return lax.dot_general(a, b, (((1,), (1,)), ((), ())), precision=precision,
                           preferred_element_type=_F32)


def _resident(shape, slab=None):
    if slab is None:
        block, at = tuple(shape), (0,) * len(shape)
    else:
        block, at = (None,) + tuple(shape[1:]), (slab,) + (0,) * (len(shape) - 1)
    return pl.BlockSpec(block, lambda *_: at, pipeline_mode=pl.Buffered(1))


def _params(*semantics):
    return pltpu.CompilerParams(dimension_semantics=semantics,
                                vmem_limit_bytes=V7X_VMEM_LIMIT_BYTES)


def _subtile_rows(tm):
    ts = min(SUBTILE, tm)
    assert tm % ts == 0
    return [slice(r * ts, (r + 1) * ts) for r in range(tm // ts)]


def _ffn_kernel(*refs, project_attention):
    if project_attention:
        h_ref, gpre_ref, gpost_ref, wg_ref, wu_ref, wd_ref, attn_ref, wo_ref, gmix_ref, o_ref = refs
    else:
        h_ref, gpre_ref, gpost_ref, wg_ref, wu_ref, wd_ref, o_ref = refs
    tiles = _subtile_rows(h_ref.shape[0])
    live = [dict() for _ in tiles]

    def load(r):
        x = h_ref[tiles[r], :]
        if project_attention:
            x = x + _rms(_dot(attn_ref[tiles[r], :], wo_ref[...]), gmix_ref[...])
            o_ref[tiles[r], :] = x
        live[r]["xn"] = _rms(x, gpre_ref[...]).astype(_BF16)

    def expand(r):
        xn = live[r].pop("xn")
        g = _dot(xn, wg_ref[...])
        u = _dot(xn, wu_ref[...])
        live[r]["a"] = (g * jax.nn.sigmoid(g) * u).astype(_BF16)

    def contract(r):
        live[r]["f"] = _dot(live[r].pop("a"), wd_ref[...])

    def store(r):
        x = (o_ref if project_attention else h_ref)[tiles[r], :]
        o_ref[tiles[r], :] = x + 0.5 * _rms(live[r].pop("f"), gpost_ref[...])

    load(0)
    expand(0)
    for r in range(len(tiles)):
        if r + 1 < len(tiles):
            load(r + 1)
        contract(r)
        if r + 1 < len(tiles):
            expand(r + 1)
        store(r)


def _ffn(h, g_pre, g_post, wg, wu, wd, layer, attention=None):
    n, d = h.shape
    tm = min(TOKEN_TILE, n)
    tok = pl.BlockSpec((tm, d), lambda i: (i, 0))
    in_specs = [tok, _resident((1, d)), _resident((1, d)),
                _resident(wg.shape, layer), _resident(wu.shape, layer), _resident(wd.shape, layer)]
    args = [h, g_pre, g_post, wg, wu, wd]
    if attention is not None:
        attn_out, w_o, slab, g_mix = attention
        in_specs += [tok, _resident(w_o.shape, slab), _resident((1, d))]
        args += [attn_out, w_o, g_mix]
    return pl.pallas_call(
        functools.partial(_ffn_kernel, project_attention=attention is not None),
        grid=(n // tm,),
        in_specs=in_specs,
        out_specs=tok,
        out_shape=jax.ShapeDtypeStruct((n, d), _F32),
        compiler_params=_params("parallel"),
        name="ffn_attn_proj" if attention is not None else "ffn",
    )(*args)


def _pool_sgu_kernel(h_ref, gpre_ref, gpost_ref, win_ref, poolw_ref, pscale_ref, sgun_ref,
                     sguw_ref, sgubt_ref, wout_ref, o_ref, sums, ybuf):
    t = pl.program_id(1)
    tm = h_ref.shape[1]
    tiles = _subtile_rows(tm)
    ts = tm // len(tiles)
    gd = poolw_ref.shape[-1]
    pool_w = len(POOL_WINDOWS) * gd
    sgu_w = SGU_GROUPS * gd
    live = [dict() for _ in tiles]
    causal = (lax.broadcasted_iota(jnp.int32, (SGU_CHUNK, SGU_CHUNK), 0)
              >= lax.broadcasted_iota(jnp.int32, (SGU_CHUNK, SGU_CHUNK), 1))

    @pl.when(t == 0)
    def _():
        sums[:, 0:POOL_HALO, :] = jnp.zeros((sums.shape[0], POOL_HALO, pool_w), _F32)

    def project_in(r):
        z = _dot(_rms(h_ref[0, tiles[r], :], gpre_ref[...]).astype(_BF16), win_ref[...])
        sums[0, POOL_HALO + r * ts:POOL_HALO + (r + 1) * ts, :] = z[:, :pool_w]
        live[r].update(u=z[:, pool_w:pool_w + sgu_w], v=z[:, pool_w + sgu_w:])

    def mix(r):
        base = POOL_HALO + r * ts
        pos = t * tm + r * ts + lax.broadcasted_iota(jnp.int32, (ts, 1), 0)
        for g, win in enumerate(POOL_WINDOWS):
            lanes = slice(g * gd, (g + 1) * gd)
            a = sums[0, base:base + ts, lanes]
            s = a
            for k in range(win.bit_length() - 1):
                s = s + sums[k, base - 2 ** k:base - 2 ** k + ts, lanes]
                if 2 ** (k + 1) < win:
                    sums[k + 1, base:base + ts, lanes] = s
            inv_count = 1.0 / jnp.minimum(pos + 1, win).astype(_F32)
            d = (s * inv_count - a).astype(_BF16)
            ybuf[tiles[r], lanes] = (_dot(d, poolw_ref[g]) * pscale_ref[g:g + 1, :]).astype(_BF16)
        u = _gelu(live[r].pop("u"))
        v = _gelu(live[r].pop("v"))
        for g in range(SGU_GROUPS):
            lanes = slice(g * gd, (g + 1) * gd)
            vn = _rms(v[:, lanes], sgun_ref[g:g + 1, :]).astype(_BF16)
            w = jnp.where(causal, sguw_ref[g], 0.0).astype(_BF16)
            bias = sgubt_ref[:, g:g + 1]
            for c in range(ts // SGU_CHUNK):
                rows = slice(c * SGU_CHUNK, (c + 1) * SGU_CHUNK)
                s = _dot(w, vn[rows]) + bias
                ybuf[r * ts + c * SGU_CHUNK:r * ts + (c + 1) * SGU_CHUNK,
                     pool_w + g * gd:pool_w + (g + 1) * gd] = (u[rows, lanes] * s).astype(_BF16)

    def project_out(r):
        m = _dot(ybuf[tiles[r], :], wout_ref[...])
        o_ref[0, tiles[r], :] = h_ref[0, tiles[r], :] + _rms(m, gpost_ref[...])

    project_in(0)
    for r in range(len(tiles)):
        if r + 1 < len(tiles):
            project_in(r + 1)
        mix(r)
        project_out(r)
    sums[:, 0:POOL_HALO, :] = sums[:, tm:tm + POOL_HALO, :]


def _pool_sgu(h, g_pre, g_post, w_in, pool_w, pool_scale, sgu_norm, sgu_w, sgu_bt, w_out, slab):
    b, s, d = h.shape
    tm = min(TOKEN_TILE, s)
    assert s % tm == 0 and min(SUBTILE, tm) % SGU_CHUNK == 0 and min(SUBTILE, tm) >= POOL_HALO
    g, gd = pool_scale.shape
    y_w = w_out.shape[1]
    assert all(w == 2 ** (w.bit_length() - 1) and w <= POOL_HALO for w in POOL_WINDOWS)
    n_sums = max(POOL_WINDOWS).bit_length() - 1
    tok = pl.BlockSpec((1, tm, d), lambda i, t: (i, t, 0))
    return pl.pallas_call(
        _pool_sgu_kernel,
        grid=(b, s // tm),
        in_specs=[tok, _resident((1, d)), _resident((1, d)), _resident(w_in.shape, slab),
                  _resident(pool_w.shape, slab), _resident(pool_scale.shape), _resident(sgu_norm.shape),
                  _resident(sgu_w.shape), _resident(sgu_bt.shape), _resident(w_out.shape, slab)],
        out_specs=tok,
        out_shape=jax.ShapeDtypeStruct((b, s, d), _F32),
        scratch_shapes=[pltpu.VMEM((n_sums, POOL_HALO + tm, g * gd), _F32), pltpu.VMEM((tm, y_w), _BF16)],
        compiler_params=_params("arbitrary", "arbitrary"),
        name="pool_sgu",
    )(h, g_pre, g_post, w_in, pool_w, pool_scale, sgu_norm, sgu_w, sgu_bt, w_out)


def _qkv_kernel(h_ref, gpre_ref, w_ref, cos_ref, sin_lo_ref, sin_hi_ref, q_ref, k_ref, vt_ref, km_ref):
    tm, d = h_ref.shape[1], h_ref.shape[2]
    hd = d // N_HEADS
    half = hd // ROT_DIM_FRACTION // 2
    tiles = _subtile_rows(tm)
    ts = tm // len(tiles)
    live = [dict() for _ in tiles]

    def project(r):
        live[r]["qkv"] = _dot(_rms(h_ref[0, tiles[r], :], gpre_ref[...]).astype(_BF16), w_ref[...])

    def emit(r):
        qkv = live[r].pop("qkv")
        cos, sin_lo, sin_hi = cos_ref[tiles[r], :], sin_lo_ref[tiles[r], :], sin_hi_ref[tiles[r], :]

        def rotary(xh):
            return (xh * cos + pltpu.roll(xh, hd - half, 1) * sin_lo + pltpu.roll(xh, half, 1) * sin_hi)

        for h in range(N_HEADS):
            lanes = slice(h * hd, (h + 1) * hd)
            q_ref[0, h, tiles[r], :] = rotary(qkv[:, h * hd:(h + 1) * hd])
            kr = rotary(qkv[:, d + h * hd:d + (h + 1) * hd])
            k_ref[0, h, tiles[r], :] = kr.astype(_BF16)
            vt_ref[0, h, :, tiles[r]] = qkv[:, 2 * d + h * hd:2 * d + (h + 1) * hd].T.astype(_BF16)
            for blk in range(ts // MOBA_BLOCK):
                rows = slice(blk * MOBA_BLOCK, (blk + 1) * MOBA_BLOCK)
                km_ref[0, r * (ts // MOBA_BLOCK) + blk, :, lanes] = jnp.mean(kr[rows], axis=0, keepdims=True)

    project(0)
    for r in range(len(tiles)):
        if r + 1 < len(tiles):
            project(r + 1)
        emit(r)


def _qkv(h, g_pre, w_qkv, slab, cos, sin_lo, sin_hi):
    b, s, d = h.shape
    hd = d // N_HEADS
    tm = min(QKV_TILE, s)
    assert s % tm == 0 and min(SUBTILE, tm) % MOBA_BLOCK == 0
    nb_tile = tm // MOBA_BLOCK
    tab = pl.BlockSpec((tm, hd), lambda i, t: (t, 0))
    return pl.pallas_call(
        _qkv_kernel,
        grid=(b, s // tm),
        in_specs=[pl.BlockSpec((1, tm, d), lambda i, t: (i, t, 0)), _resident((1, d)),
                  _resident(w_qkv.shape, slab), tab, tab, tab],
        out_specs=[pl.BlockSpec((1, N_HEADS, tm, hd), lambda i, t: (i, 0, t, 0)),
                   pl.BlockSpec((1, N_HEADS, tm, hd), lambda i, t: (i, 0, t, 0)),
                   pl.BlockSpec((1, N_HEADS, hd, tm), lambda i, t: (i, 0, 0, t)),
                   pl.BlockSpec((1, nb_tile, 1, d), lambda i, t: (i, t, 0, 0))],
        out_shape=[jax.ShapeDtypeStruct((b, N_HEADS, s, hd), _F32),
                   jax.ShapeDtypeStruct((b, N_HEADS, s, hd), _BF16),
                   jax.ShapeDtypeStruct((b, N_HEADS, hd, s), _BF16),
                   jax.ShapeDtypeStruct((b, s // MOBA_BLOCK, 1, d), _F32)],
        compiler_params=_params("parallel", "parallel"),
        name="qkv_rotary",
    )(h, g_pre, w_qkv, cos, sin_lo, sin_hi)


def _moba_kernel(q_ref, k_ref, vt_ref, km_ref, onehot_ref, o_ref, *t_refs):
    blk = MOBA_BLOCK
    sub = F32_SUBLANES
    km = km_ref[0]
    nb, hd = km.shape
    to_log2 = (hd ** -0.5) * LOG2_E
    causal = (lax.broadcasted_iota(jnp.int32, (blk, blk), 0) <= lax.broadcasted_iota(jnp.int32, (blk, blk), 1))
    state = [dict() for _ in range(nb)]

    def rows(j, n=1):
        return slice(j * blk, (j + n) * blk)

    def groups(c):
        return [(j, min(KV_GROUP, c + 1 - j)) for j in range(0, c + 1, KV_GROUP)]

    def gate_step(c):
        q32 = q_ref[0, 0, rows(c), :]
        gate = _dot_nt(km, q32, precision=lax.Precision.HIGHEST)
        bid = lax.broadcasted_iota(jnp.int32, gate.shape, 0)
        past = bid < c
        gate = jnp.where(past, gate, NEG_INF)
        rank = jnp.zeros(gate.shape, jnp.int32)
        for m in range(nb):
            row = gate[m:m + 1, :]
            rank = rank + ((row > gate) | ((row == gate) & (m < bid))).astype(jnp.int32)
        bias = jnp.where(past & (rank >= min(MOBA_TOPK, nb)), NEG_INF, 0.0)
        bias = jnp.concatenate([bias, jnp.zeros((hd - nb, blk), _F32)], axis=0).T
        qa = jnp.concatenate([(q32 * to_log2).astype(_BF16), bias.astype(_BF16)], axis=1)
        state[c].update(qa=qa, m=None, acc=None)

    def score_step(c, j, n):
        st = state[c]
        ka = jnp.concatenate([k_ref[0, 0, rows(j, n), :], onehot_ref[rows(j, n), :]], axis=1)
        s = _dot_nt(ka, st["qa"])
        for r in range(n):
            sr = s[r * blk:(r + 1) * blk]
            if j + r == c:
                sr = jnp.where(causal, sr, NEG_INF)
            t_refs[c % SCORE_BUFFERS][rows(j + r), :] = sr
            part = jnp.max(sr.reshape(blk // sub, sub, blk), axis=0)
            st["m"] = part if st["m"] is None else jnp.maximum(st["m"], part)

    def softmax_step(c, j, n):
        st = state[c]
        if st["acc"] is None:
            st["m"] = jnp.max(st["m"], axis=0, keepdims=True)
        p = jnp.exp2((t_refs[c % SCORE_BUFFERS][rows(j, n), :] - st["m"]).astype(_BF16))
        vta = jnp.concatenate([vt_ref[0, 0, :, rows(j, n)], jnp.ones((BF16_SUBLANES, n * blk), _BF16)], axis=0)
        pv = _dot(vta, p)
        st["acc"] = pv if st["acc"] is None else st["acc"] + pv

    def finish(c):
        acc = state[c]["acc"]
        inv_l = 1.0 / acc[hd:hd + 1, :]
        o_ref[0, rows(c), :] = (acc[0:hd] * inv_l).T.astype(o_ref.dtype)
        state[c].clear()

    done_before = lambda c: c * (c + 1) // 2
    schedule = []
    for c in range(nb):
        schedule.append((done_before(c) - GATE_LEAD, 0, ("gate", c), functools.partial(gate_step, c)))
        for j, n in groups(c):
            schedule.append((done_before(c) + j, 1, ("score", c), functools.partial(score_step, c, j, n)))
            schedule.append((done_before(c + 1) + SOFTMAX_LAG + j, 2, ("softmax", c),
                             functools.partial(softmax_step, c, j, n)))
        schedule.append((done_before(c + 1) + SOFTMAX_LAG + c + 0.5, 3, ("finish", c), functools.partial(finish, c)))
    schedule.sort(key=lambda e: e[:2])
    tags = [e[2] for e in schedule]
    for c in range(nb - SCORE_BUFFERS):
        last_read = max(i for i, tag in enumerate(tags) if tag == ("softmax", c))
        assert last_read < tags.index(("score", c + SCORE_BUFFERS))
    for *_, step in schedule:
        step()


def _moba(q, k, vt, kmean):
    b, nh, s, hd = q.shape
    nb = s // MOBA_BLOCK
    assert nb <= hd
    onehot = (jnp.arange(s)[:, None] // MOBA_BLOCK == jnp.arange(hd)[None, :]).astype(_BF16)
    return pl.pallas_call(
        _moba_kernel,
        grid=(b, nh),
        in_specs=[pl.BlockSpec((1, 1, s, hd), lambda i, h: (i, h, 0, 0)),
                  pl.BlockSpec((1, 1, s, hd), lambda i, h: (i, h, 0, 0)),
                  pl.BlockSpec((1, 1, hd, s), lambda i, h: (i, h, 0, 0)),
                  pl.BlockSpec((1, nb, hd), lambda i, h: (i, 0, h)), _resident((s, hd))],
        out_specs=pl.BlockSpec((1, s, hd), lambda i, h: (i, 0, h)),
        out_shape=jax.ShapeDtypeStruct((b, s, nh * hd), _BF16),
        scratch_shapes=[pltpu.VMEM((s, MOBA_BLOCK), _F32)] * SCORE_BUFFERS,
        compiler_params=_params("parallel", "parallel"),
        name="moba_attention",
    )(q, k, vt, kmean, onehot)


def _rotary_tables(s, hd):
    rot = hd // ROT_DIM_FRACTION
    half = rot // 2
    pos = jnp.arange(s, dtype=_F32)
    inv_freq = 1.0 / (ROPE_THETA ** (jnp.arange(0, rot, 2, dtype=_F32) / rot))
    ang = pos[:, None] * inv_freq[None, :]
    cos, sin = jnp.cos(ang), jnp.sin(ang)
    zeros = jnp.zeros((s, hd - rot), _F32)
    zhalf = jnp.zeros((s, half), _F32)
    return (jnp.concatenate([cos, cos, jnp.ones((s, hd - rot), _F32)], axis=1),
            jnp.concatenate([-sin, zhalf, zeros], axis=1),
            jnp.concatenate([zhalf, sin, zeros], axis=1))


def kernel(x, ffn1_w_gate, ffn1_w_up, ffn1_w_down, ffn2_w_gate, ffn2_w_up, ffn2_w_down, norm_pre, norm_post, ab_w_in, pool_w, pool_scale, sgu_norm, sgu_w, sgu_b, ab_w_out, attn_w_qkv, attn_w_o):
    b, s, d = x.shape
    depth = ffn1_w_gate.shape[0]
    ffn1 = [w.astype(_BF16) for w in (ffn1_w_gate, ffn1_w_up, ffn1_w_down)]
    ffn2 = [w.astype(_BF16) for w in (ffn2_w_gate, ffn2_w_up, ffn2_w_down)]
    ab_w_in, pool_w, ab_w_out, attn_w_qkv, attn_w_o = (
        w.astype(_BF16) for w in (ab_w_in, pool_w, ab_w_out, attn_w_qkv, attn_w_o))
    tables = _rotary_tables(s, d // N_HEADS)
    h = x.reshape(b * s, d)
    for layer in range(depth):
        pre, post = norm_pre[layer], norm_post[layer]
        h = _ffn(h, pre[0:1], post[0:1], *ffn1, layer)
        i = layer // 2
        attention = None
        if layer % 2 == 0:
            h = _pool_sgu(h.reshape(b, s, d), pre[1:2], post[1:2], ab_w_in, pool_w, pool_scale[i],
                          sgu_norm[i], sgu_w[i], sgu_b[i].T, ab_w_out, i).reshape(b * s, d)
        else:
            q, k, vt, kmean = _qkv(h.reshape(b, s, d), pre[1:2], attn_w_qkv, i, *tables)
            o = _moba(q, k, vt, kmean.reshape(b, s // MOBA_BLOCK, d))
            attention = (o.reshape(b * s, d), attn_w_o, i, post[1:2])
        h = _ffn(h, pre[2:3], post[2:3], *ffn2, layer, attention=attention)
    return h.reshape(b, s, d)
```

```python
import functools

import jax
import jax.numpy as jnp
from jax import lax
from jax.experimental import pallas as pl
from jax.experimental.pallas import tpu as pltpu

NORM_EPS = 1e-6
POOL_WINDOWS = (2, 4, 8, 16)
POOL_HALO = 16
SGU_GROUPS = 4
SGU_CHUNK = 128
N_HEADS = 8
ROT_DIM_FRACTION = 4
ROPE_THETA = 500000.0
MOBA_BLOCK = 256
MOBA_TOPK = 3
NEG_INF = -1e30
LOG2_E = 1.4426950408889634

V7X_VMEM_LIMIT_BYTES = 56 * 1024 * 1024
F32_SUBLANES = 8
BF16_SUBLANES = 16
TOKEN_TILE = 1024
SUBTILE = 256
QKV_TILE = 1024
KV_GROUP = 4
SOFTMAX_LAG = 8
GATE_LEAD = 10
SCORE_BUFFERS = 4

_BF16 = jnp.bfloat16
_F32 = jnp.float32


def _rms(x, g):
    return x * lax.rsqrt(jnp.mean(x * x, axis=-1, keepdims=True) + NORM_EPS) * g


def _gelu(x):
    return 0.5 * x * (1.0 + lax.erf(x * (0.5 ** 0.5)))


def _dot(a, b):
    return jnp.dot(a, b, preferred_element_type=_F32)


def _dot_nt(a, b, precision=None):
    return lax.dot_general(a, b, (((1,), (1,)), ((), ())), precision=precision,
                           preferred_element_type=_F32)


def _resident(shape, slab=None):
    if slab is None:
        block, at = tuple(shape), (0,) * len(shape)
    else:
        block, at = (None,) + tuple(shape[1:]), (slab,) + (0,) * (len(shape) - 1)
    return pl.BlockSpec(block, lambda *_: at, pipeline_mode=pl.Buffered(1))


def _params(*semantics):
    return pltpu.CompilerParams(dimension_semantics=semantics,
                                vmem_limit_bytes=V7X_VMEM_LIMIT_BYTES)


def _subtile_rows(tm):
    ts = min(SUBTILE, tm)
    assert tm % ts == 0
    return [slice(r * ts, (r + 1) * ts) for r in range(tm // ts)]


def _ffn_kernel(*refs, project_attention):
    if project_attention:
        h_ref, gpre_ref, gpost_ref, wg_ref, wu_ref, wd_ref, attn_ref, wo_ref, gmix_ref, o_ref = refs
    else:
        h_ref, gpre_ref, gpost_ref, wg_ref, wu_ref, wd_ref, o_ref = refs
    tiles = _subtile_rows(h_ref.shape[0])
    live = [dict() for _ in tiles]

    def load(r):
        x = h_ref[tiles[r], :]
        if project_attention:
            x = x + _rms(_dot(attn_ref[tiles[r], :], wo_ref[...]), gmix_ref[...])
            o_ref[tiles[r], :] = x
        live[r]["xn"] = _rms(x, gpre_ref[...]).astype(_BF16)

    def expand(r):
        xn = live[r].pop("xn")
        g = _dot(xn, wg_ref[...])
        u = _dot(xn, wu_ref[...])
        live[r]["a"] = (g * jax.nn.sigmoid(g) * u).astype(_BF16)

    def contract(r):
        live[r]["f"] = _dot(live[r].pop("a"), wd_ref[...])

    def store(r):
        x = (o_ref if project_attention else h_ref)[tiles[r], :]
        o_ref[tiles[r], :] = x + 0.5 * _rms(live[r].pop("f"), gpost_ref[...])

    load(0)
    expand(0)
    for r in range(len(tiles)):
        if r + 1 < len(tiles):
            load(r + 1)
        contract(r)
        if r + 1 < len(tiles):
            expand(r + 1)
        store(r)


def _ffn(h, g_pre, g_post, wg, wu, wd, layer, attention=None):
    n, d = h.shape
    tm = min(TOKEN_TILE, n)
    tok = pl.BlockSpec((tm, d), lambda i: (i, 0))
    in_specs = [tok, _resident((1, d)), _resident((1, d)),
                _resident(wg.shape, layer), _resident(wu.shape, layer), _resident(wd.shape, layer)]
    args = [h, g_pre, g_post, wg, wu, wd]
    if attention is not None:
        attn_out, w_o, slab, g_mix = attention
        in_specs += [tok, _resident(w_o.shape, slab), _resident((1, d))]
        args += [attn_out, w_o, g_mix]
    return pl.pallas_call(
        functools.partial(_ffn_kernel, project_attention=attention is not None),
        grid=(n // tm,),
        in_specs=in_specs,
        out_specs=tok,
        out_shape=jax.ShapeDtypeStruct((n, d), _F32),
        compiler_params=_params("parallel"),
        name="ffn_attn_proj" if attention is not None else "ffn",
    )(*args)


def _pool_sgu_kernel(h_ref, gpre_ref, gpost_ref, win_ref, poolw_ref, pscale_ref, sgun_ref,
                     sguw_ref, sgubt_ref, wout_ref, o_ref, sums, ybuf):
    t = pl.program_id(1)
    tm = h_ref.shape[1]
    tiles = _subtile_rows(tm)
    ts = tm // len(tiles)
    gd = poolw_ref.shape[-1]
    pool_w = len(POOL_WINDOWS) * gd
    sgu_w = SGU_GROUPS * gd
    live = [dict() for _ in tiles]
    causal = (lax.broadcasted_iota(jnp.int32, (SGU_CHUNK, SGU_CHUNK), 0)
              >= lax.broadcasted_iota(jnp.int32, (SGU_CHUNK, SGU_CHUNK), 1))

    @pl.when(t == 0)
    def _():
        sums[:, 0:POOL_HALO, :] = jnp.zeros((sums.shape[0], POOL_HALO, pool_w), _F32)

    def project_in(r):
        z = _dot(_rms(h_ref[0, tiles[r], :], gpre_ref[...]).astype(_BF16), win_ref[...])
        sums[0, POOL_HALO + r * ts:POOL_HALO + (r + 1) * ts, :] = z[:, :pool_w]
        live[r].update(u=z[:, pool_w:pool_w + sgu_w], v=z[:, pool_w + sgu_w:])

    def mix(r):
        base = POOL_HALO + r * ts
        pos = t * tm + r * ts + lax.broadcasted_iota(jnp.int32, (ts, 1), 0)
        for g, win in enumerate(POOL_WINDOWS):
            lanes = slice(g * gd, (g + 1) * gd)
            a = sums[0, base:base + ts, lanes]
            s = a
            for k in range(win.bit_length() - 1):
                s = s + sums[k, base - 2 ** k:base - 2 ** k + ts, lanes]
                if 2 ** (k + 1) < win:
                    sums[k + 1, base:base + ts, lanes] = s
            inv_count = 1.0 / jnp.minimum(pos + 1, win).astype(_F32)
            d = (s * inv_count - a).astype(_BF16)
            ybuf[tiles[r], lanes] = (_dot(d, poolw_ref[g]) * pscale_ref[g:g + 1, :]).astype(_BF16)
        u = _gelu(live[r].pop("u"))
        v = _gelu(live[r].pop("v"))
        for g in range(SGU_GROUPS):
            lanes = slice(g * gd, (g + 1) * gd)
            vn = _rms(v[:, lanes], sgun_ref[g:g + 1, :]).astype(_BF16)
            w = jnp.where(causal, sguw_ref[g], 0.0).astype(_BF16)
            bias = sgubt_ref[:, g:g + 1]
            for c in range(ts // SGU_CHUNK):
                rows = slice(c * SGU_CHUNK, (c + 1) * SGU_CHUNK)
                s = _dot(w, vn[rows]) + bias
                ybuf[r * ts + c * SGU_CHUNK:r * ts + (c + 1) * SGU_CHUNK,
                     pool_w + g * gd:pool_w + (g + 1) * gd] = (u[rows, lanes] * s).astype(_BF16)

    def project_out(r):
        m = _dot(ybuf[tiles[r], :], wout_ref[...])
        o_ref[0, tiles[r], :] = h_ref[0, tiles[r], :] + _rms(m, gpost_ref[...])

    project_in(0)
    for r in range(len(tiles)):
        if r + 1 < len(tiles):
            project_in(r + 1)
        mix(r)
        project_out(r)
    sums[:, 0:POOL_HALO, :] = sums[:, tm:tm + POOL_HALO, :]


def _pool_sgu(h, g_pre, g_post, w_in, pool_w, pool_scale, sgu_norm, sgu_w, sgu_bt, w_out, slab):
    b, s, d = h.shape
    tm = min(TOKEN_TILE, s)
    assert s % tm == 0 and min(SUBTILE, tm) % SGU_CHUNK == 0 and min(SUBTILE, tm) >= POOL_HALO
    g, gd = pool_scale.shape
    y_w = w_out.shape[1]
    assert all(w == 2 ** (w.bit_length() - 1) and w <= POOL_HALO for w in POOL_WINDOWS)
    n_sums = max(POOL_WINDOWS).bit_length() - 1
    tok = pl.BlockSpec((1, tm, d), lambda i, t: (i, t, 0))
    return pl.pallas_call(
        _pool_sgu_kernel,
        grid=(b, s // tm),
        in_specs=[tok, _resident((1, d)), _resident((1, d)), _resident(w_in.shape, slab),
                  _resident(pool_w.shape, slab), _resident(pool_scale.shape), _resident(sgu_norm.shape),
                  _resident(sgu_w.shape), _resident(sgu_bt.shape), _resident(w_out.shape, slab)],
        out_specs=tok,
        out_shape=jax.ShapeDtypeStruct((b, s, d), _F32),
        scratch_shapes=[pltpu.VMEM((n_sums, POOL_HALO + tm, g * gd), _F32), pltpu.VMEM((tm, y_w), _BF16)],
        compiler_params=_params("arbitrary", "arbitrary"),
        name="pool_sgu",
    )(h, g_pre, g_post, w_in, pool_w, pool_scale, sgu_norm, sgu_w, sgu_bt, w_out)


def _qkv_kernel(h_ref, gpre_ref, w_ref, cos_ref, sin_lo_ref, sin_hi_ref, q_ref, k_ref, vt_ref, km_ref):
    tm, d = h_ref.shape[1], h_ref.shape[2]
    hd = d // N_HEADS
    half = hd // ROT_DIM_FRACTION // 2
    tiles = _subtile_rows(tm)
    ts = tm // len(tiles)
    live = [dict() for _ in tiles]

    def project(r):
        live[r]["qkv"] = _dot(_rms(h_ref[0, tiles[r], :], gpre_ref[...]).astype(_BF16), w_ref[...])

    def emit(r):
        qkv = live[r].pop("qkv")
        cos, sin_lo, sin_hi = cos_ref[tiles[r], :], sin_lo_ref[tiles[r], :], sin_hi_ref[tiles[r], :]

        def rotary(xh):
            return (xh * cos + pltpu.roll(xh, hd - half, 1) * sin_lo + pltpu.roll(xh, half, 1) * sin_hi)

        for h in range(N_HEADS):
            lanes = slice(h * hd, (h + 1) * hd)
            q_ref[0, h, tiles[r], :] = rotary(qkv[:, h * hd:(h + 1) * hd])
            kr = rotary(qkv[:, d + h * hd:d + (h + 1) * hd])
            k_ref[0, h, tiles[r], :] = kr.astype(_BF16)
            vt_ref[0, h, :, tiles[r]] = qkv[:, 2 * d + h * hd:2 * d + (h + 1) * hd].T.astype(_BF16)
            for blk in range(ts // MOBA_BLOCK):
                rows = slice(blk * MOBA_BLOCK, (blk + 1) * MOBA_BLOCK)
                km_ref[0, r * (ts // MOBA_BLOCK) + blk, :, lanes] = jnp.mean(kr[rows], axis=0, keepdims=True)

    project(0)
    for r in range(len(tiles)):
        if r + 1 < len(tiles):
            project(r + 1)
        emit(r)


def _qkv(h, g_pre, w_qkv, slab, cos, sin_lo, sin_hi):
    b, s, d = h.shape
    hd = d // N_HEADS
    tm = min(QKV_TILE, s)
    assert s % tm == 0 and min(SUBTILE, tm) % MOBA_BLOCK == 0
    nb_tile = tm // MOBA_BLOCK
    tab = pl.BlockSpec((tm, hd), lambda i, t: (t, 0))
    return pl.pallas_call(
        _qkv_kernel,
        grid=(b, s // tm),
        in_specs=[pl.BlockSpec((1, tm, d), lambda i, t: (i, t, 0)), _resident((1, d)),
                  _resident(w_qkv.shape, slab), tab, tab, tab],
        out_specs=[pl.BlockSpec((1, N_HEADS, tm, hd), lambda i, t: (i, 0, t, 0)),
                   pl.BlockSpec((1, N_HEADS, tm, hd), lambda i, t: (i, 0, t, 0)),
                   pl.BlockSpec((1, N_HEADS, hd, tm), lambda i, t: (i, 0, 0, t)),
                   pl.BlockSpec((1, nb_tile, 1, d), lambda i, t: (i, t, 0, 0))],
        out_shape=[jax.ShapeDtypeStruct((b, N_HEADS, s, hd), _F32),
                   jax.ShapeDtypeStruct((b, N_HEADS, s, hd), _BF16),
                   jax.ShapeDtypeStruct((b, N_HEADS, hd, s), _BF16),
                   jax.ShapeDtypeStruct((b, s // MOBA_BLOCK, 1, d), _F32)],
        compiler_params=_params("parallel", "parallel"),
        name="qkv_rotary",
    )(h, g_pre, w_qkv, cos, sin_lo, sin_hi)


def _moba_kernel(q_ref, k_ref, vt_ref, km_ref, onehot_ref, o_ref, *t_refs):
    blk = MOBA_BLOCK
    sub = F32_SUBLANES
    km = km_ref[0]
    nb, hd = km.shape
    to_log2 = (hd ** -0.5) * LOG2_E
    causal = (lax.broadcasted_iota(jnp.int32, (blk, blk), 0) <= lax.broadcasted_iota(jnp.int32, (blk, blk), 1))
    state = [dict() for _ in range(nb)]

    def rows(j, n=1):
        return slice(j * blk, (j + n) * blk)

    def groups(c):
        return [(j, min(KV_GROUP, c + 1 - j)) for j in range(0, c + 1, KV_GROUP)]

    def gate_step(c):
        q32 = q_ref[0, 0, rows(c), :]
        gate = _dot_nt(km, q32, precision=lax.Precision.HIGHEST)
        bid = lax.broadcasted_iota(jnp.int32, gate.shape, 0)
        past = bid < c
        gate = jnp.where(past, gate, NEG_INF)
        rank = jnp.zeros(gate.shape, jnp.int32)
        for m in range(nb):
            row = gate[m:m + 1, :]
            rank = rank + ((row > gate) | ((row == gate) & (m < bid))).astype(jnp.int32)
        bias = jnp.where(past & (rank >= min(MOBA_TOPK, nb)), NEG_INF, 0.0)
        bias = jnp.concatenate([bias, jnp.zeros((hd - nb, blk), _F32)], axis=0).T
        qa = jnp.concatenate([(q32 * to_log2).astype(_BF16), bias.astype(_BF16)], axis=1)
        state[c].update(qa=qa, m=None, acc=None)

    def score_step(c, j, n):
        st = state[c]
        ka = jnp.concatenate([k_ref[0, 0, rows(j, n), :], onehot_ref[rows(j, n), :]], axis=1)
        s = _dot_nt(ka, st["qa"])
        for r in range(n):
            sr = s[r * blk:(r + 1) * blk]
            if j + r == c:
                sr = jnp.where(causal, sr, NEG_INF)
            t_refs[c % SCORE_BUFFERS][rows(j + r), :] = sr
            part = jnp.max(sr.reshape(blk // sub, sub, blk), axis=0)
            st["m"] = part if st["m"] is None else jnp.maximum(st["m"], part)

    def softmax_step(c, j, n):
        st = state[c]
        if st["acc"] is None:
            st["m"] = jnp.max(st["m"], axis=0, keepdims=True)
        p = jnp.exp2((t_refs[c % SCORE_BUFFERS][rows(j, n), :] - st["m"]).astype(_BF16))
        vta = jnp.concatenate([vt_ref[0, 0, :, rows(j, n)], jnp.ones((BF16_SUBLANES, n * blk), _BF16)], axis=0)
        pv = _dot(vta, p)
        st["acc"] = pv if st["acc"] is None else st["acc"] + pv

    def finish(c):
        acc = state[c]["acc"]
        inv_l = 1.0 / acc[hd:hd + 1, :]
        o_ref[0, rows(c), :] = (acc[0:hd] * inv_l).T.astype(o_ref.dtype)
        state[c].clear()

    done_before = lambda c: c * (c + 1) // 2
    schedule = []
    for c in range(nb):
        schedule.append((done_before(c) - GATE_LEAD, 0, ("gate", c), functools.partial(gate_step, c)))
        for j, n in groups(c):
            schedule.append((done_before(c) + j, 1, ("score", c), functools.partial(score_step, c, j, n)))
            schedule.append((done_before(c + 1) + SOFTMAX_LAG + j, 2, ("softmax", c),
                             functools.partial(softmax_step, c, j, n)))
        schedule.append((done_before(c + 1) + SOFTMAX_LAG + c + 0.5, 3, ("finish", c), functools.partial(finish, c)))
    schedule.sort(key=lambda e: e[:2])
    tags = [e[2] for e in schedule]
    for c in range(nb - SCORE_BUFFERS):
        last_read = max(i for i, tag in enumerate(tags) if tag == ("softmax", c))
        assert last_read < tags.index(("score", c + SCORE_BUFFERS))
    for *_, step in schedule:
        step()


def _moba(q, k, vt, kmean):
    b, nh, s, hd = q.shape
    nb = s // MOBA_BLOCK
    assert nb <= hd
    onehot = (jnp.arange(s)[:, None] // MOBA_BLOCK == jnp.arange(hd)[None, :]).astype(_BF16)
    return pl.pallas_call(
        _moba_kernel,
        grid=(b, nh),
        in_specs=[pl.BlockSpec((1, 1, s, hd), lambda i, h: (i, h, 0, 0)),
                  pl.BlockSpec((1, 1, s, hd), lambda i, h: (i, h, 0, 0)),
                  pl.BlockSpec((1, 1, hd, s), lambda i, h: (i, h, 0, 0)),
                  pl.BlockSpec((1, nb, hd), lambda i, h: (i, 0, h)), _resident((s, hd))],
        out_specs=pl.BlockSpec((1, s, hd), lambda i, h: (i, 0, h)),
        out_shape=jax.ShapeDtypeStruct((b, s, nh * hd), _BF16),
        scratch_shapes=[pltpu.VMEM((s, MOBA_BLOCK), _F32)] * SCORE_BUFFERS,
        compiler_params=_params("parallel", "parallel"),
        name="moba_attention",
    )(q, k, vt, kmean, onehot)


def _rotary_tables(s, hd):
    rot = hd // ROT_DIM_FRACTION
    half = rot // 2
    pos = jnp.arange(s, dtype=_F32)
    inv_freq = 1.0 / (ROPE_THETA ** (jnp.arange(0, rot, 2, dtype=_F32) / rot))
    ang = pos[:, None] * inv_freq[None, :]
    cos, sin = jnp.cos(ang), jnp.sin(ang)
    zeros = jnp.zeros((s, hd - rot), _F32)
    zhalf = jnp.zeros((s, half), _F32)
    return (jnp.concatenate([cos, cos, jnp.ones((s, hd - rot), _F32)], axis=1),
            jnp.concatenate([-sin, zhalf, zeros], axis=1),
            jnp.concatenate([zhalf, sin, zeros], axis=1))


def kernel(x, ffn1_w_gate, ffn1_w_up, ffn1_w_down, ffn2_w_gate, ffn2_w_up, ffn2_w_down, norm_pre, norm_post, ab_w_in, pool_w, pool_scale, sgu_norm, sgu_w, sgu_b, ab_w_out, attn_w_qkv, attn_w_o):
    b, s, d = x.shape
    depth = ffn1_w_gate.shape[0]
    ffn1 = [w.astype(_BF16) for w in (ffn1_w_gate, ffn1_w_up, ffn1_w_down)]
    ffn2 = [w.astype(_BF16) for w in (ffn2_w_gate, ffn2_w_up, ffn2_w_down)]
    ab_w_in, pool_w, ab_w_out, attn_w_qkv, attn_w_o = (
        w.astype(_BF16) for w in (ab_w_in, pool_w, ab_w_out, attn_w_qkv, attn_w_o))
    tables = _rotary_tables(s, d // N_HEADS)
    h = x.reshape(b * s, d)
    for layer in range(depth):
        pre, post = norm_pre[layer], norm_post[layer]
        h = _ffn(h, pre[0:1], post[0:1], *ffn1, layer)
        i = layer // 2
        attention = None
        if layer % 2 == 0:
            h = _pool_sgu(h.reshape(b, s, d), pre[1:2], post[1:2], ab_w_in, pool_w, pool_scale[i],
                          sgu_norm[i], sgu_w[i], sgu_b[i].T, ab_w_out, i).reshape(b * s, d)
        else:
            q, k, vt, kmean = _qkv(h.reshape(b, s, d), pre[1:2], attn_w_qkv, i, *tables)
            o = _moba(q, k, vt, kmean.reshape(b, s // MOBA_BLOCK, d))
            attention = (o.reshape(b * s, d), attn_w_o, i, post[1:2])
        h = _ffn(h, pre[2:3], post[2:3], *ffn2, layer, attention=attention)
    return h.reshape(b, s, d)
```

```python
import functools

import jax
import jax.numpy as jnp
from jax import lax
from jax.experimental import pallas as pl
from jax.experimental.pallas import tpu as pltpu

NORM_EPS = 1e-6
POOL_WINDOWS = (2, 4, 8, 16)
POOL_HALO = 16
SGU_GROUPS = 4
SGU_CHUNK = 128
N_HEADS = 8
ROT_DIM_FRACTION = 4
ROPE_THETA = 500000.0
MOBA_BLOCK = 256
MOBA_TOPK = 3
NEG_INF = -1e30
LOG2_E = 1.4426950408889634

V7X_VMEM_LIMIT_BYTES = 56 * 1024 * 1024
F32_SUBLANES = 8
BF16_SUBLANES = 16
TOKEN_TILE = 1024
SUBTILE = 256
QKV_TILE = 1024
KV_GROUP = 4
SOFTMAX_LAG = 8
GATE_LEAD = 10
SCORE_BUFFERS = 4

_BF16 = jnp.bfloat16
_F32 = jnp.float32


def _rms(x, g):
    return x * lax.rsqrt(jnp.mean(x * x, axis=-1, keepdims=True) + NORM_EPS) * g


def _gelu(x):
    return 0.5 * x * (1.0 + lax.erf(x * (0.5 ** 0.5)))


def _dot(a, b):
    return jnp.dot(a, b, preferred_element_type=_F32)


def _dot_nt(a, b, precision=None):
    return lax.dot_general(a, b, (((1,), (1,)), ((), ())), precision=precision,
                           preferred_element_type=_F32)


def _resident(shape, slab=None):
    if slab is None:
        block, at = tuple(shape), (0,) * len(shape)
    else:
        block, at = (None,) + tuple(shape[1:]), (slab,) + (0,) * (len(shape) - 1)
    return pl.BlockSpec(block, lambda *_: at, pipeline_mode=pl.Buffered(1))


def _params(*semantics):
    return pltpu.CompilerParams(dimension_semantics=semantics,
                                vmem_limit_bytes=V7X_VMEM_LIMIT_BYTES)


def _subtile_rows(tm):
    ts = min(SUBTILE, tm)
    assert tm % ts == 0
    return [slice(r * ts, (r + 1) * ts) for r in range(tm // ts)]


def _ffn_kernel(*refs, project_attention):
    if project_attention:
        h_ref, gpre_ref, gpost_ref, wg_ref, wu_ref, wd_ref, attn_ref, wo_ref, gmix_ref, o_ref = refs
    else:
        h_ref, gpre_ref, gpost_ref, wg_ref, wu_ref, wd_ref, o_ref = refs
    tiles = _subtile_rows(h_ref.shape[0])
    live = [dict() for _ in tiles]

    def load(r):
        x = h_ref[tiles[r], :]
        if project_attention:
            x = x + _rms(_dot(attn_ref[tiles[r], :], wo_ref[...]), gmix_ref[...])
            o_ref[tiles[r], :] = x
        live[r]["xn"] = _rms(x, gpre_ref[...]).astype(_BF16)

    def expand(r):
        xn = live[r].pop("xn")
        g = _dot(xn, wg_ref[...])
        u = _dot(xn, wu_ref[...])
        live[r]["a"] = (g * jax.nn.sigmoid(g) * u).astype(_BF16)

    def contract(r):
        live[r]["f"] = _dot(live[r].pop("a"), wd_ref[...])

    def store(r):
        x = (o_ref if project_attention else h_ref)[tiles[r], :]
        o_ref[tiles[r], :] = x + 0.5 * _rms(live[r].pop("f"), gpost_ref[...])

    load(0)
    expand(0)
    for r in range(len(tiles)):
        if r + 1 < len(tiles):
            load(r + 1)
        contract(r)
        if r + 1 < len(tiles):
            expand(r + 1)
        store(r)


def _ffn(h, g_pre, g_post, wg, wu, wd, layer, attention=None):
    n, d = h.shape
    tm = min(TOKEN_TILE, n)
    tok = pl.BlockSpec((tm, d), lambda i: (i, 0))
    in_specs = [tok, _resident((1, d)), _resident((1, d)),
                _resident(wg.shape, layer), _resident(wu.shape, layer), _resident(wd.shape, layer)]
    args = [h, g_pre, g_post, wg, wu, wd]
    if attention is not None:
        attn_out, w_o, slab, g_mix = attention
        in_specs += [tok, _resident(w_o.shape, slab), _resident((1, d))]
        args += [attn_out, w_o, g_mix]
    return pl.pallas_call(
        functools.partial(_ffn_kernel, project_attention=attention is not None),
        grid=(n // tm,),
        in_specs=in_specs,
        out_specs=tok,
        out_shape=jax.ShapeDtypeStruct((n, d), _F32),
        compiler_params=_params("parallel"),
        name="ffn_attn_proj" if attention is not None else "ffn",
    )(*args)


def _pool_sgu_kernel(h_ref, gpre_ref, gpost_ref, win_ref, poolw_ref, pscale_ref, sgun_ref,
                     sguw_ref, sgubt_ref, wout_ref, o_ref, sums, ybuf):
    t = pl.program_id(1)
    tm = h_ref.shape[1]
    tiles = _subtile_rows(tm)
    ts = tm // len(tiles)
    gd = poolw_ref.shape[-1]
    pool_w = len(POOL_WINDOWS) * gd
    sgu_w = SGU_GROUPS * gd
    live = [dict() for _ in tiles]
    causal = (lax.broadcasted_iota(jnp.int32, (SGU_CHUNK, SGU_CHUNK), 0)
              >= lax.broadcasted_iota(jnp.int32, (SGU_CHUNK, SGU_CHUNK), 1))

    @pl.when(t == 0)
    def _():
        sums[:, 0:POOL_HALO, :] = jnp.zeros((sums.shape[0], POOL_HALO, pool_w), _F32)

    def project_in(r):
        z = _dot(_rms(h_ref[0, tiles[r], :], gpre_ref[...]).astype(_BF16), win_ref[...])
        sums[0, POOL_HALO + r * ts:POOL_HALO + (r + 1) * ts, :] = z[:, :pool_w]
        live[r].update(u=z[:, pool_w:pool_w + sgu_w], v=z[:, pool_w + sgu_w:])

    def mix(r):
        base = POOL_HALO + r * ts
        pos = t * tm + r * ts + lax.broadcasted_iota(jnp.int32, (ts, 1), 0)
        for g, win in enumerate(POOL_WINDOWS):
            lanes = slice(g * gd, (g + 1) * gd)
            a = sums[0, base:base + ts, lanes]
            s = a
            for k in range(win.bit_length() - 1):
                s = s + sums[k, base - 2 ** k:base - 2 ** k + ts, lanes]
                if 2 ** (k + 1) < win:
                    sums[k + 1, base:base + ts, lanes] = s
            inv_count = 1.0 / jnp.minimum(pos + 1, win).astype(_F32)
            d = (s * inv_count - a).astype(_BF16)
            ybuf[tiles[r], lanes] = (_dot(d, poolw_ref[g]) * pscale_ref[g:g + 1, :]).astype(_BF16)
        u = _gelu(live[r].pop("u"))
        v = _gelu(live[r].pop("v"))
        for g in range(SGU_GROUPS):
            lanes = slice(g * gd, (g + 1) * gd)
            vn = _rms(v[:, lanes], sgun_ref[g:g + 1, :]).astype(_BF16)
            w = jnp.where(causal, sguw_ref[g], 0.0).astype(_BF16)
            bias = sgubt_ref[:, g:g + 1]
            for c in range(ts // SGU_CHUNK):
                rows = slice(c * SGU_CHUNK, (c + 1) * SGU_CHUNK)
                s = _dot(w, vn[rows]) + bias
                ybuf[r * ts + c * SGU_CHUNK:r * ts + (c + 1) * SGU_CHUNK,
                     pool_w + g * gd:pool_w + (g + 1) * gd] = (u[rows, lanes] * s).astype(_BF16)

    def project_out(r):
        m = _dot(ybuf[tiles[r], :], wout_ref[...])
        o_ref[0, tiles[r], :] = h_ref[0, tiles[r], :] + _rms(m, gpost_ref[...])

    project_in(0)
    for r in range(len(tiles)):
        if r + 1 < len(tiles):
            project_in(r + 1)
        mix(r)
        project_out(r)
    sums[:, 0:POOL_HALO, :] = sums[:, tm:tm + POOL_HALO, :]


def _pool_sgu(h, g_pre, g_post, w_in, pool_w, pool_scale, sgu_norm, sgu_w, sgu_bt, w_out, slab):
    b, s, d = h.shape
    tm = min(TOKEN_TILE, s)
    assert s % tm == 0 and min(SUBTILE, tm) % SGU_CHUNK == 0 and min(SUBTILE, tm) >= POOL_HALO
    g, gd = pool_scale.shape
    y_w = w_out.shape[1]
    assert all(w == 2 ** (w.bit_length() - 1) and w <= POOL_HALO for w in POOL_WINDOWS)
    n_sums = max(POOL_WINDOWS).bit_length() - 1
    tok = pl.BlockSpec((1, tm, d), lambda i, t: (i, t, 0))
    return pl.pallas_call(
        _pool_sgu_kernel,
        grid=(b, s // tm),
        in_specs=[tok, _resident((1, d)), _resident((1, d)), _resident(w_in.shape, slab),
                  _resident(pool_w.shape, slab), _resident(pool_scale.shape), _resident(sgu_norm.shape),
                  _resident(sgu_w.shape), _resident(sgu_bt.shape), _resident(w_out.shape, slab)],
        out_specs=tok,
        out_shape=jax.ShapeDtypeStruct((b, s, d), _F32),
        scratch_shapes=[pltpu.VMEM((n_sums, POOL_HALO + tm, g * gd), _F32), pltpu.VMEM((tm, y_w), _BF16)],
        compiler_params=_params("arbitrary", "arbitrary"),
        name="pool_sgu",
    )(h, g_pre, g_post, w_in, pool_w, pool_scale, sgu_norm, sgu_w, sgu_bt, w_out)


def _qkv_kernel(h_ref, gpre_ref, w_ref, cos_ref, sin_lo_ref, sin_hi_ref, q_ref, k_ref, vt_ref, km_ref):
    tm, d = h_ref.shape[1], h_ref.shape[2]
    hd = d // N_HEADS
    half = hd // ROT_DIM_FRACTION // 2
    tiles = _subtile_rows(tm)
    ts = tm // len(tiles)
    live = [dict() for _ in tiles]

    def project(r):
        live[r]["qkv"] = _dot(_rms(h_ref[0, tiles[r], :], gpre_ref[...]).astype(_BF16), w_ref[...])

    def emit(r):
        qkv = live[r].pop("qkv")
        cos, sin_lo, sin_hi = cos_ref[tiles[r], :], sin_lo_ref[tiles[r], :], sin_hi_ref[tiles[r], :]

        def rotary(xh):
            return (xh * cos + pltpu.roll(xh, hd - half, 1) * sin_lo + pltpu.roll(xh, half, 1) * sin_hi)

        for h in range(N_HEADS):
            lanes = slice(h * hd, (h + 1) * hd)
            q_ref[0, h, tiles[r], :] = rotary(qkv[:, h * hd:(h + 1) * hd])
            kr = rotary(qkv[:, d + h * hd:d + (h + 1) * hd])
            k_ref[0, h, tiles[r], :] = kr.astype(_BF16)
            vt_ref[0, h, :, tiles[r]] = qkv[:, 2 * d + h * hd:2 * d + (h + 1) * hd].T.astype(_BF16)
            for blk in range(ts // MOBA_BLOCK):
                rows = slice(blk * MOBA_BLOCK, (blk + 1) * MOBA_BLOCK)
                km_ref[0, r * (ts // MOBA_BLOCK) + blk, :, lanes] = jnp.mean(kr[rows], axis=0, keepdims=True)

    project(0)
    for r in range(len(tiles)):
        if r + 1 < len(tiles):
            project(r + 1)
        emit(r)


def _qkv(h, g_pre, w_qkv, slab, cos, sin_lo, sin_hi):
    b, s, d = h.shape
    hd = d // N_HEADS
    tm = min(QKV_TILE, s)
    assert s % tm == 0 and min(SUBTILE, tm) % MOBA_BLOCK == 0
    nb_tile = tm // MOBA_BLOCK
    tab = pl.BlockSpec((tm, hd), lambda i, t: (t, 0))
    return pl.pallas_call(
        _qkv_kernel,
        grid=(b, s // tm),
        in_specs=[pl.BlockSpec((1, tm, d), lambda i, t: (i, t, 0)), _resident((1, d)),
                  _resident(w_qkv.shape, slab), tab, tab, tab],
        out_specs=[pl.BlockSpec((1, N_HEADS, tm, hd), lambda i, t: (i, 0, t, 0)),
                   pl.BlockSpec((1, N_HEADS, tm, hd), lambda i, t: (i, 0, t, 0)),
                   pl.BlockSpec((1, N_HEADS, hd, tm), lambda i, t: (i, 0, 0, t)),
                   pl.BlockSpec((1, nb_tile, 1, d), lambda i, t: (i, t, 0, 0))],
        out_shape=[jax.ShapeDtypeStruct((b, N_HEADS, s, hd), _F32),
                   jax.ShapeDtypeStruct((b, N_HEADS, s, hd), _BF16),
                   jax.ShapeDtypeStruct((b, N_HEADS, hd, s), _BF16),
                   jax.ShapeDtypeStruct((b, s // MOBA_BLOCK, 1, d), _F32)],
        compiler_params=_params("parallel", "parallel"),
        name="qkv_rotary",
    )(h, g_pre, w_qkv, cos, sin_lo, sin_hi)


def _moba_kernel(q_ref, k_ref, vt_ref, km_ref, onehot_ref, o_ref, *t_refs):
    blk = MOBA_BLOCK
    sub = F32_SUBLANES
    km = km_ref[0]
    nb, hd = km.shape
    to_log2 = (hd ** -0.5) * LOG2_E
    causal = (lax.broadcasted_iota(jnp.int32, (blk, blk), 0) <= lax.broadcasted_iota(jnp.int32, (blk, blk), 1))
    state = [dict() for _ in range(nb)]

    def rows(j, n=1):
        return slice(j * blk, (j + n) * blk)

    def groups(c):
        return [(j, min(KV_GROUP, c + 1 - j)) for j in range(0, c + 1, KV_GROUP)]

    def gate_step(c):
        q32 = q_ref[0, 0, rows(c), :]
        gate = _dot_nt(km, q32, precision=lax.Precision.HIGHEST)
        bid = lax.broadcasted_iota(jnp.int32, gate.shape, 0)
        past = bid < c
        gate = jnp.where(past, gate, NEG_INF)
        rank = jnp.zeros(gate.shape, jnp.int32)
        for m in range(nb):
            row = gate[m:m + 1, :]
            rank = rank + ((row > gate) | ((row == gate) & (m < bid))).astype(jnp.int32)
        bias = jnp.where(past & (rank >= min(MOBA_TOPK, nb)), NEG_INF, 0.0)
        bias = jnp.concatenate([bias, jnp.zeros((hd - nb, blk), _F32)], axis=0).T
        qa = jnp.concatenate([(q32 * to_log2).astype(_BF16), bias.astype(_BF16)], axis=1)
        state[c].update(qa=qa, m=None, acc=None)

    def score_step(c, j, n):
        st = state[c]
        ka = jnp.concatenate([k_ref[0, 0, rows(j, n), :], onehot_ref[rows(j, n), :]], axis=1)
        s = _dot_nt(ka, st["qa"])
        for r in range(n):
            sr = s[r * blk:(r + 1) * blk]
            if j + r == c:
                sr = jnp.where(causal, sr, NEG_INF)
            t_refs[c % SCORE_BUFFERS][rows(j + r), :] = sr.astype(_BF16)
            part = jnp.max(sr.reshape(blk // sub, sub, blk), axis=0)
            st["m"] = part if st["m"] is None else jnp.maximum(st["m"], part)

    def softmax_step(c, j, n):
        st = state[c]
        if st["acc"] is None:
            st["m"] = jnp.max(st["m"], axis=0, keepdims=True)
        p = jnp.exp2(t_refs[c % SCORE_BUFFERS][rows(j, n), :] - st["m"].astype(_BF16))
        vta = jnp.concatenate([vt_ref[0, 0, :, rows(j, n)], jnp.ones((BF16_SUBLANES, n * blk), _BF16)], axis=0)
        pv = _dot(vta, p)
        st["acc"] = pv if st["acc"] is None else st["acc"] + pv

    def finish(c):
        acc = state[c]["acc"]
        inv_l = 1.0 / acc[hd:hd + 1, :]
        o_ref[0, rows(c), :] = (acc[0:hd] * inv_l).T.astype(o_ref.dtype)
        state[c].clear()

    done_before = lambda c: c * (c + 1) // 2
    schedule = []
    for c in range(nb):
        schedule.append((done_before(c) - GATE_LEAD, 0, ("gate", c), functools.partial(gate_step, c)))
        for j, n in groups(c):
            schedule.append((done_before(c) + j, 1, ("score", c), functools.partial(score_step, c, j, n)))
            schedule.append((done_before(c + 1) + SOFTMAX_LAG + j, 2, ("softmax", c),
                             functools.partial(softmax_step, c, j, n)))
        schedule.append((done_before(c + 1) + SOFTMAX_LAG + c + 0.5, 3, ("finish", c), functools.partial(finish, c)))
    schedule.sort(key=lambda e: e[:2])
    tags = [e[2] for e in schedule]
    for c in range(nb - SCORE_BUFFERS):
        last_read = max(i for i, tag in enumerate(tags) if tag == ("softmax", c))
        assert last_read < tags.index(("score", c + SCORE_BUFFERS))
    for *_, step in schedule:
        step()


def _moba(q, k, vt, kmean):
    b, nh, s, hd = q.shape
    nb = s // MOBA_BLOCK
    assert nb <= hd
    onehot = (jnp.arange(s)[:, None] // MOBA_BLOCK == jnp.arange(hd)[None, :]).astype(_BF16)
    return pl.pallas_call(
        _moba_kernel,
        grid=(b, nh),
        in_specs=[pl.BlockSpec((1, 1, s, hd), lambda i, h: (i, h, 0, 0)),
                  pl.BlockSpec((1, 1, s, hd), lambda i, h: (i, h, 0, 0)),
                  pl.BlockSpec((1, 1, hd, s), lambda i, h: (i, h, 0, 0)),
                  pl.BlockSpec((1, nb, hd), lambda i, h: (i, 0, h)), _resident((s, hd))],
        out_specs=pl.BlockSpec((1, s, hd), lambda i, h: (i, 0, h)),
        out_shape=jax.ShapeDtypeStruct((b, s, nh * hd), _BF16),
        scratch_shapes=[pltpu.VMEM((s, MOBA_BLOCK), _BF16)] * SCORE_BUFFERS,
        compiler_params=_params("parallel", "parallel"),
        name="moba_attention",
    )(q, k, vt, kmean, onehot)


def _rotary_tables(s, hd):
    rot = hd // ROT_DIM_FRACTION
    half = rot // 2
    pos = jnp.arange(s, dtype=_F32)
    inv_freq = 1.0 / (ROPE_THETA ** (jnp.arange(0, rot, 2, dtype=_F32) / rot))
    ang = pos[:, None] * inv_freq[None, :]
    cos, sin = jnp.cos(ang), jnp.sin(ang)
    zeros = jnp.zeros((s, hd - rot), _F32)
    zhalf = jnp.zeros((s, half), _F32)
    return (jnp.concatenate([cos, cos, jnp.ones((s, hd - rot), _F32)], axis=1),
            jnp.concatenate([-sin, zhalf, zeros], axis=1),
            jnp.concatenate([zhalf, sin, zeros], axis=1))


def kernel(x, ffn1_w_gate, ffn1_w_up, ffn1_w_down, ffn2_w_gate, ffn2_w_up, ffn2_w_down, norm_pre, norm_post, ab_w_in, pool_w, pool_scale, sgu_norm, sgu_w, sgu_b, ab_w_out, attn_w_qkv, attn_w_o):
    b, s, d = x.shape
    depth = ffn1_w_gate.shape[0]
    ffn1 = [w.astype(_BF16) for w in (ffn1_w_gate, ffn1_w_up, ffn1_w_down)]
    ffn2 = [w.astype(_BF16) for w in (ffn2_w_gate, ffn2_w_up, ffn2_w_down)]
    ab_w_in, pool_w, ab_w_out, attn_w_qkv, attn_w_o = (
        w.astype(_BF16) for w in (ab_w_in, pool_w, ab_w_out, attn_w_qkv, attn_w_o))
    tables = _rotary_tables(s, d // N_HEADS)
    h = x.reshape(b * s, d)
    for layer in range(depth):
        pre, post = norm_pre[layer], norm_post[layer]
        h = _ffn(h, pre[0:1], post[0:1], *ffn1, layer)
        i = layer // 2
        attention = None
        if layer % 2 == 0:
            h = _pool_sgu(h.reshape(b, s, d), pre[1:2], post[1:2], ab_w_in, pool_w, pool_scale[i],
                          sgu_norm[i], sgu_w[i], sgu_b[i].T, ab_w_out, i).reshape(b * s, d)
        else:
            q, k, vt, kmean = _qkv(h.reshape(b, s, d), pre[1:2], attn_w_qkv, i, *tables)
            o = _moba(q, k, vt, kmean.reshape(b, s // MOBA_BLOCK, d))
            attention = (o.reshape(b * s, d), attn_w_o, i, post[1:2])
        h = _ffn(h, pre[2:3], post[2:3], *ffn2, layer, attention=attention)
    return h.reshape(b, s, d)
```

```python
import functools

import jax
import jax.numpy as jnp
from jax import lax
from jax.experimental import pallas as pl
from jax.experimental.pallas import tpu as pltpu

NORM_EPS = 1e-6
POOL_WINDOWS = (2, 4, 8, 16)
POOL_HALO = 16
SGU_GROUPS = 4
SGU_CHUNK = 128
N_HEADS = 8
ROT_DIM_FRACTION = 4
ROPE_THETA = 500000.0
MOBA_BLOCK = 256
MOBA_TOPK = 3
NEG_INF = -1e30
LOG2_E = 1.4426950408889634

V7X_VMEM_LIMIT_BYTES = 56 * 1024 * 1024
F32_SUBLANES = 8
BF16_SUBLANES = 16
TOKEN_TILE = 1024
SUBTILE = 256
QKV_TILE = 1024
KV_GROUP = 4
SOFTMAX_LAG = 8
GATE_LEAD = 10
SCORE_BUFFERS = 4

_BF16 = jnp.bfloat16
_F32 = jnp.float32


def _rms(x, g):
    return x * lax.rsqrt(jnp.mean(x * x, axis=-1, keepdims=True) + NORM_EPS) * g


def _gelu(x):
    return 0.5 * x * (1.0 + lax.erf(x * (0.5 ** 0.5)))


def _dot(a, b):
    return jnp.dot(a, b, preferred_element_type=_F32)


def _dot_nt(a, b, precision=None):
    return lax.dot_general(a, b, (((1,), (1,)), ((), ())), precision=precision,
                           preferred_element_type=_F32)


def _resident(shape, slab=None):
    if slab is None:
        block, at = tuple(shape), (0,) * len(shape)
    else:
        block, at = (None,) + tuple(shape[1:]), (slab,) + (0,) * (len(shape) - 1)
    return pl.BlockSpec(block, lambda *_: at, pipeline_mode=pl.Buffered(1))


def _params(*semantics):
    return pltpu.CompilerParams(dimension_semantics=semantics,
                                vmem_limit_bytes=V7X_VMEM_LIMIT_BYTES)


def _subtile_rows(tm):
    ts = min(SUBTILE, tm)
    assert tm % ts == 0
    return [slice(r * ts, (r + 1) * ts) for r in range(tm // ts)]


def _ffn_kernel(*refs, project_attention):
    if project_attention:
        h_ref, gpre_ref, gpost_ref, wg_ref, wu_ref, wd_ref, attn_ref, wo_ref, gmix_ref, o_ref = refs
    else:
        h_ref, gpre_ref, gpost_ref, wg_ref, wu_ref, wd_ref, o_ref = refs
    tiles = _subtile_rows(h_ref.shape[0])
    live = [dict() for _ in tiles]

    def load(r):
        x = h_ref[tiles[r], :]
        if project_attention:
            heads = jnp.concatenate([attn_ref[0, hh, tiles[r], :] for hh in range(attn_ref.shape[1])], axis=1)
            x = x + _rms(_dot(heads, wo_ref[...]), gmix_ref[...])
            o_ref[tiles[r], :] = x
        live[r]["xn"] = _rms(x, gpre_ref[...]).astype(_BF16)

    def expand(r):
        xn = live[r].pop("xn")
        g = _dot(xn, wg_ref[...])
        u = _dot(xn, wu_ref[...])
        live[r]["a"] = (g * jax.nn.sigmoid(g) * u).astype(_BF16)

    def contract(r):
        live[r]["f"] = _dot(live[r].pop("a"), wd_ref[...])

    def store(r):
        x = (o_ref if project_attention else h_ref)[tiles[r], :]
        o_ref[tiles[r], :] = x + 0.5 * _rms(live[r].pop("f"), gpost_ref[...])

    load(0)
    expand(0)
    for r in range(len(tiles)):
        if r + 1 < len(tiles):
            load(r + 1)
        contract(r)
        if r + 1 < len(tiles):
            expand(r + 1)
        store(r)


def _ffn(h, g_pre, g_post, wg, wu, wd, layer, attention=None):
    n, d = h.shape
    tm = min(TOKEN_TILE, n)
    tok = pl.BlockSpec((tm, d), lambda i: (i, 0))
    in_specs = [tok, _resident((1, d)), _resident((1, d)),
                _resident(wg.shape, layer), _resident(wu.shape, layer), _resident(wd.shape, layer)]
    args = [h, g_pre, g_post, wg, wu, wd]
    if attention is not None:
        attn_out, w_o, slab, g_mix = attention
        _, nh, s, hd = attn_out.shape
        assert s % tm == 0
        heads = pl.BlockSpec((1, nh, tm, hd), lambda i: (i // (s // tm), 0, i % (s // tm), 0))
        in_specs += [heads, _resident(w_o.shape, slab), _resident((1, d))]
        args += [attn_out, w_o, g_mix]
    return pl.pallas_call(
        functools.partial(_ffn_kernel, project_attention=attention is not None),
        grid=(n // tm,),
        in_specs=in_specs,
        out_specs=tok,
        out_shape=jax.ShapeDtypeStruct((n, d), _F32),
        compiler_params=_params("parallel"),
        name="ffn_attn_proj" if attention is not None else "ffn",
    )(*args)


def _pool_sgu_kernel(h_ref, gpre_ref, gpost_ref, win_ref, poolw_ref, pscale_ref, sgun_ref,
                     sguw_ref, sgubt_ref, wout_ref, o_ref, sums, ybuf):
    t = pl.program_id(1)
    tm = h_ref.shape[1]
    tiles = _subtile_rows(tm)
    ts = tm // len(tiles)
    gd = poolw_ref.shape[-1]
    pool_w = len(POOL_WINDOWS) * gd
    sgu_w = SGU_GROUPS * gd
    live = [dict() for _ in tiles]
    causal = (lax.broadcasted_iota(jnp.int32, (SGU_CHUNK, SGU_CHUNK), 0)
              >= lax.broadcasted_iota(jnp.int32, (SGU_CHUNK, SGU_CHUNK), 1))

    @pl.when(t == 0)
    def _():
        sums[:, 0:POOL_HALO, :] = jnp.zeros((sums.shape[0], POOL_HALO, pool_w), _F32)

    def project_in(r):
        z = _dot(_rms(h_ref[0, tiles[r], :], gpre_ref[...]).astype(_BF16), win_ref[...])
        sums[0, POOL_HALO + r * ts:POOL_HALO + (r + 1) * ts, :] = z[:, :pool_w]
        live[r].update(u=z[:, pool_w:pool_w + sgu_w], v=z[:, pool_w + sgu_w:])

    def mix(r):
        base = POOL_HALO + r * ts
        pos = t * tm + r * ts + lax.broadcasted_iota(jnp.int32, (ts, 1), 0)
        for g, win in enumerate(POOL_WINDOWS):
            lanes = slice(g * gd, (g + 1) * gd)
            a = sums[0, base:base + ts, lanes]
            s = a
            for k in range(win.bit_length() - 1):
                s = s + sums[k, base - 2 ** k:base - 2 ** k + ts, lanes]
                if 2 ** (k + 1) < win:
                    sums[k + 1, base:base + ts, lanes] = s
            inv_count = 1.0 / jnp.minimum(pos + 1, win).astype(_F32)
            d = (s * inv_count - a).astype(_BF16)
            ybuf[tiles[r], lanes] = (_dot(d, poolw_ref[g]) * pscale_ref[g:g + 1, :]).astype(_BF16)
        u = _gelu(live[r].pop("u"))
        v = _gelu(live[r].pop("v"))
        for g in range(SGU_GROUPS):
            lanes = slice(g * gd, (g + 1) * gd)
            vn = _rms(v[:, lanes], sgun_ref[g:g + 1, :]).astype(_BF16)
            w = jnp.where(causal, sguw_ref[g], 0.0).astype(_BF16)
            bias = sgubt_ref[:, g:g + 1]
            for c in range(ts // SGU_CHUNK):
                rows = slice(c * SGU_CHUNK, (c + 1) * SGU_CHUNK)
                s = _dot(w, vn[rows]) + bias
                ybuf[r * ts + c * SGU_CHUNK:r * ts + (c + 1) * SGU_CHUNK,
                     pool_w + g * gd:pool_w + (g + 1) * gd] = (u[rows, lanes] * s).astype(_BF16)

    def project_out(r):
        m = _dot(ybuf[tiles[r], :], wout_ref[...])
        o_ref[0, tiles[r], :] = h_ref[0, tiles[r], :] + _rms(m, gpost_ref[...])

    project_in(0)
    for r in range(len(tiles)):
        if r + 1 < len(tiles):
            project_in(r + 1)
        mix(r)
        project_out(r)
    sums[:, 0:POOL_HALO, :] = sums[:, tm:tm + POOL_HALO, :]


def _pool_sgu(h, g_pre, g_post, w_in, pool_w, pool_scale, sgu_norm, sgu_w, sgu_bt, w_out, slab):
    b, s, d = h.shape
    tm = min(TOKEN_TILE, s)
    assert s % tm == 0 and min(SUBTILE, tm) % SGU_CHUNK == 0 and min(SUBTILE, tm) >= POOL_HALO
    g, gd = pool_scale.shape
    y_w = w_out.shape[1]
    assert all(w == 2 ** (w.bit_length() - 1) and w <= POOL_HALO for w in POOL_WINDOWS)
    n_sums = max(POOL_WINDOWS).bit_length() - 1
    tok = pl.BlockSpec((1, tm, d), lambda i, t: (i, t, 0))
    return pl.pallas_call(
        _pool_sgu_kernel,
        grid=(b, s // tm),
        in_specs=[tok, _resident((1, d)), _resident((1, d)), _resident(w_in.shape, slab),
                  _resident(pool_w.shape, slab), _resident(pool_scale.shape), _resident(sgu_norm.shape),
                  _resident(sgu_w.shape), _resident(sgu_bt.shape), _resident(w_out.shape, slab)],
        out_specs=tok,
        out_shape=jax.ShapeDtypeStruct((b, s, d), _F32),
        scratch_shapes=[pltpu.VMEM((n_sums, POOL_HALO + tm, g * gd), _F32), pltpu.VMEM((tm, y_w), _BF16)],
        compiler_params=_params("arbitrary", "arbitrary"),
        name="pool_sgu",
    )(h, g_pre, g_post, w_in, pool_w, pool_scale, sgu_norm, sgu_w, sgu_bt, w_out)


def _qkv_kernel(h_ref, gpre_ref, w_ref, cos_ref, sin_lo_ref, sin_hi_ref, q_ref, k_ref, vt_ref, km_ref):
    tm, d = h_ref.shape[1], h_ref.shape[2]
    hd = d // N_HEADS
    half = hd // ROT_DIM_FRACTION // 2
    tiles = _subtile_rows(tm)
    ts = tm // len(tiles)
    live = [dict() for _ in tiles]

    def project(r):
        live[r]["qkv"] = _dot(_rms(h_ref[0, tiles[r], :], gpre_ref[...]).astype(_BF16), w_ref[...])

    def emit(r):
        qkv = live[r].pop("qkv")
        cos, sin_lo, sin_hi = cos_ref[tiles[r], :], sin_lo_ref[tiles[r], :], sin_hi_ref[tiles[r], :]

        def rotary(xh):
            return (xh * cos + pltpu.roll(xh, hd - half, 1) * sin_lo + pltpu.roll(xh, half, 1) * sin_hi)

        for h in range(N_HEADS):
            lanes = slice(h * hd, (h + 1) * hd)
            q_ref[0, h, tiles[r], :] = rotary(qkv[:, h * hd:(h + 1) * hd])
            kr = rotary(qkv[:, d + h * hd:d + (h + 1) * hd])
            k_ref[0, h, tiles[r], :] = kr.astype(_BF16)
            vt_ref[0, h, :, tiles[r]] = qkv[:, 2 * d + h * hd:2 * d + (h + 1) * hd].T.astype(_BF16)
            for blk in range(ts // MOBA_BLOCK):
                rows = slice(blk * MOBA_BLOCK, (blk + 1) * MOBA_BLOCK)
                km_ref[0, r * (ts // MOBA_BLOCK) + blk, :, lanes] = jnp.mean(kr[rows], axis=0, keepdims=True)

    project(0)
    for r in range(len(tiles)):
        if r + 1 < len(tiles):
            project(r + 1)
        emit(r)


def _qkv(h, g_pre, w_qkv, slab, cos, sin_lo, sin_hi):
    b, s, d = h.shape
    hd = d // N_HEADS
    tm = min(QKV_TILE, s)
    assert s % tm == 0 and min(SUBTILE, tm) % MOBA_BLOCK == 0
    nb_tile = tm // MOBA_BLOCK
    tab = pl.BlockSpec((tm, hd), lambda i, t: (t, 0))
    return pl.pallas_call(
        _qkv_kernel,
        grid=(b, s // tm),
        in_specs=[pl.BlockSpec((1, tm, d), lambda i, t: (i, t, 0)), _resident((1, d)),
                  _resident(w_qkv.shape, slab), tab, tab, tab],
        out_specs=[pl.BlockSpec((1, N_HEADS, tm, hd), lambda i, t: (i, 0, t, 0)),
                   pl.BlockSpec((1, N_HEADS, tm, hd), lambda i, t: (i, 0, t, 0)),
                   pl.BlockSpec((1, N_HEADS, hd, tm), lambda i, t: (i, 0, 0, t)),
                   pl.BlockSpec((1, nb_tile, 1, d), lambda i, t: (i, t, 0, 0))],
        out_shape=[jax.ShapeDtypeStruct((b, N_HEADS, s, hd), _F32),
                   jax.ShapeDtypeStruct((b, N_HEADS, s, hd), _BF16),
                   jax.ShapeDtypeStruct((b, N_HEADS, hd, s), _BF16),
                   jax.ShapeDtypeStruct((b, s // MOBA_BLOCK, 1, d), _F32)],
        compiler_params=_params("parallel", "parallel"),
        name="qkv_rotary",
    )(h, g_pre, w_qkv, cos, sin_lo, sin_hi)


def _moba_kernel(q_ref, k_ref, vt_ref, km_ref, onehot_ref, o_ref, *t_refs):
    blk = MOBA_BLOCK
    sub = F32_SUBLANES
    km = km_ref[0]
    nb, hd = km.shape
    to_log2 = (hd ** -0.5) * LOG2_E
    causal = (lax.broadcasted_iota(jnp.int32, (blk, blk), 0) <= lax.broadcasted_iota(jnp.int32, (blk, blk), 1))
    state = [dict() for _ in range(nb)]

    def rows(j, n=1):
        return slice(j * blk, (j + n) * blk)

    def groups(c):
        return [(j, min(KV_GROUP, c + 1 - j)) for j in range(0, c + 1, KV_GROUP)]

    def gate_step(c):
        q32 = q_ref[0, 0, rows(c), :]
        gate = _dot_nt(km, q32, precision=lax.Precision.HIGHEST)
        bid = lax.broadcasted_iota(jnp.int32, gate.shape, 0)
        past = bid < c
        gate = jnp.where(past, gate, NEG_INF)
        rank = jnp.zeros(gate.shape, jnp.int32)
        for m in range(nb):
            row = gate[m:m + 1, :]
            rank = rank + ((row > gate) | ((row == gate) & (m < bid))).astype(jnp.int32)
        bias = jnp.where(past & (rank >= min(MOBA_TOPK, nb)), NEG_INF, 0.0)
        bias = jnp.concatenate([bias, jnp.zeros((hd - nb, blk), _F32)], axis=0).T
        qa = jnp.concatenate([(q32 * to_log2).astype(_BF16), bias.astype(_BF16)], axis=1)
        state[c].update(qa=qa, m=None, acc=None)

    def score_step(c, j, n):
        st = state[c]
        ka = jnp.concatenate([k_ref[0, 0, rows(j, n), :], onehot_ref[rows(j, n), :]], axis=1)
        s = _dot_nt(ka, st["qa"])
        for r in range(n):
            sr = s[r * blk:(r + 1) * blk]
            if j + r == c:
                sr = jnp.where(causal, sr, NEG_INF)
            t_refs[c % SCORE_BUFFERS][rows(j + r), :] = sr.astype(_BF16)
            part = jnp.max(sr.reshape(blk // sub, sub, blk), axis=0)
            st["m"] = part if st["m"] is None else jnp.maximum(st["m"], part)

    def softmax_step(c, j, n):
        st = state[c]
        if st["acc"] is None:
            st["m"] = jnp.max(st["m"], axis=0, keepdims=True)
        p = jnp.exp2(t_refs[c % SCORE_BUFFERS][rows(j, n), :] - st["m"].astype(_BF16))
        vta = jnp.concatenate([vt_ref[0, 0, :, rows(j, n)], jnp.ones((BF16_SUBLANES, n * blk), _BF16)], axis=0)
        pv = _dot(vta, p)
        st["acc"] = pv if st["acc"] is None else st["acc"] + pv

    def finish(c):
        acc = state[c]["acc"]
        inv_l = 1.0 / acc[hd:hd + 1, :]
        o_ref[0, 0, rows(c), :] = (acc[0:hd] * inv_l).T.astype(o_ref.dtype)
        state[c].clear()

    done_before = lambda c: c * (c + 1) // 2
    schedule = []
    for c in range(nb):
        schedule.append((done_before(c) - GATE_LEAD, 0, ("gate", c), functools.partial(gate_step, c)))
        for j, n in groups(c):
            schedule.append((done_before(c) + j, 1, ("score", c), functools.partial(score_step, c, j, n)))
            schedule.append((done_before(c + 1) + SOFTMAX_LAG + j, 2, ("softmax", c),
                             functools.partial(softmax_step, c, j, n)))
        schedule.append((done_before(c + 1) + SOFTMAX_LAG + c + 0.5, 3, ("finish", c), functools.partial(finish, c)))
    schedule.sort(key=lambda e: e[:2])
    tags = [e[2] for e in schedule]
    for c in range(nb - SCORE_BUFFERS):
        last_read = max(i for i, tag in enumerate(tags) if tag == ("softmax", c))
        assert last_read < tags.index(("score", c + SCORE_BUFFERS))
    for *_, step in schedule:
        step()


def _moba(q, k, vt, kmean):
    b, nh, s, hd = q.shape
    nb = s // MOBA_BLOCK
    assert nb <= hd
    onehot = (jnp.arange(s)[:, None] // MOBA_BLOCK == jnp.arange(hd)[None, :]).astype(_BF16)
    return pl.pallas_call(
        _moba_kernel,
        grid=(b, nh),
        in_specs=[pl.BlockSpec((1, 1, s, hd), lambda i, h: (i, h, 0, 0)),
                  pl.BlockSpec((1, 1, s, hd), lambda i, h: (i, h, 0, 0)),
                  pl.BlockSpec((1, 1, hd, s), lambda i, h: (i, h, 0, 0)),
                  pl.BlockSpec((1, nb, hd), lambda i, h: (i, 0, h)), _resident((s, hd))],
        out_specs=pl.BlockSpec((1, 1, s, hd), lambda i, h: (i, h, 0, 0)),
        out_shape=jax.ShapeDtypeStruct((b, nh, s, hd), _BF16),
        scratch_shapes=[pltpu.VMEM((s, MOBA_BLOCK), _BF16)] * SCORE_BUFFERS,
        compiler_params=_params("parallel", "parallel"),
        name="moba_attention",
    )(q, k, vt, kmean, onehot)


def _rotary_tables(s, hd):
    rot = hd // ROT_DIM_FRACTION
    half = rot // 2
    pos = jnp.arange(s, dtype=_F32)
    inv_freq = 1.0 / (ROPE_THETA ** (jnp.arange(0, rot, 2, dtype=_F32) / rot))
    ang = pos[:, None] * inv_freq[None, :]
    cos, sin = jnp.cos(ang), jnp.sin(ang)
    zeros = jnp.zeros((s, hd - rot), _F32)
    zhalf = jnp.zeros((s, half), _F32)
    return (jnp.concatenate([cos, cos, jnp.ones((s, hd - rot), _F32)], axis=1),
            jnp.concatenate([-sin, zhalf, zeros], axis=1),
            jnp.concatenate([zhalf, sin, zeros], axis=1))


def kernel(x, ffn1_w_gate, ffn1_w_up, ffn1_w_down, ffn2_w_gate, ffn2_w_up, ffn2_w_down, norm_pre, norm_post, ab_w_in, pool_w, pool_scale, sgu_norm, sgu_w, sgu_b, ab_w_out, attn_w_qkv, attn_w_o):
    b, s, d = x.shape
    depth = ffn1_w_gate.shape[0]
    ffn1 = [w.astype(_BF16) for w in (ffn1_w_gate, ffn1_w_up, ffn1_w_down)]
    ffn2 = [w.astype(_BF16) for w in (ffn2_w_gate, ffn2_w_up, ffn2_w_down)]
    ab_w_in, pool_w, ab_w_out, attn_w_qkv, attn_w_o = (
        w.astype(_BF16) for w in (ab_w_in, pool_w, ab_w_out, attn_w_qkv, attn_w_o))
    tables = _rotary_tables(s, d // N_HEADS)
    h = x.reshape(b * s, d)
    for layer in range(depth):
        pre, post = norm_pre[layer], norm_post[layer]
        h = _ffn(h, pre[0:1], post[0:1], *ffn1, layer)
        i = layer // 2
        attention = None
        if layer % 2 == 0:
            h = _pool_sgu(h.reshape(b, s, d), pre[1:2], post[1:2], ab_w_in, pool_w, pool_scale[i],
                          sgu_norm[i], sgu_w[i], sgu_b[i].T, ab_w_out, i).reshape(b * s, d)
        else:
            q, k, vt, kmean = _qkv(h.reshape(b, s, d), pre[1:2], attn_w_qkv, i, *tables)
            o = _moba(q, k, vt, kmean.reshape(b, s // MOBA_BLOCK, d))
            attention = (o, attn_w_o, i, post[1:2])
        h = _ffn(h, pre[2:3], post[2:3], *ffn2, layer, attention=attention)
    return h.reshape(b, s, d)
```

```python
import functools

import jax
import jax.numpy as jnp
from jax import lax
from jax.experimental import pallas as pl
from jax.experimental.pallas import tpu as pltpu

NORM_EPS = 1e-6
POOL_WINDOWS = (2, 4, 8, 16)
POOL_HALO = 16
SGU_GROUPS = 4
SGU_CHUNK = 128
N_HEADS = 8
ROT_DIM_FRACTION = 4
ROPE_THETA = 500000.0
MOBA_BLOCK = 256
MOBA_TOPK = 3
NEG_INF = -1e30
LOG2_E = 1.4426950408889634

V7X_VMEM_LIMIT_BYTES = 56 * 1024 * 1024
F32_SUBLANES = 8
BF16_SUBLANES = 16
TOKEN_TILE = 1024
SUBTILE = 256
QKV_TILE = 1024
KV_GROUP = 4
SOFTMAX_LAG = 8
GATE_LEAD = 10
SCORE_BUFFERS = 4

_BF16 = jnp.bfloat16
_F32 = jnp.float32


def _rms(x, g):
    return x * lax.rsqrt(jnp.mean(x * x, axis=-1, keepdims=True) + NORM_EPS) * g


def _gelu(x):
    return 0.5 * x * (1.0 + lax.erf(x * (0.5 ** 0.5)))


def _dot(a, b):
    return jnp.dot(a, b, preferred_element_type=_F32)


def _dot_nt(a, b, precision=None):
    return lax.dot_general(a, b, (((1,), (1,)), ((), ())), precision=precision,
                           preferred_element_type=_F32)


def _resident(shape, slab=None):
    if slab is None:
        block, at = tuple(shape), (0,) * len(shape)
    else:
        block, at = (None,) + tuple(shape[1:]), (slab,) + (0,) * (len(shape) - 1)
    return pl.BlockSpec(block, lambda *_: at, pipeline_mode=pl.Buffered(1))


def _params(*semantics):
    return pltpu.CompilerParams(dimension_semantics=semantics,
                                vmem_limit_bytes=V7X_VMEM_LIMIT_BYTES)


def _subtile_rows(tm):
    ts = min(SUBTILE, tm)
    assert tm % ts == 0
    return [slice(r * ts, (r + 1) * ts) for r in range(tm // ts)]


def _ffn_kernel(*refs, project_attention):
    if project_attention:
        h_ref, gpre_ref, gpost_ref, wg_ref, wu_ref, wd_ref, attn_ref, wo_ref, gmix_ref, o_ref = refs
    else:
        h_ref, gpre_ref, gpost_ref, wg_ref, wu_ref, wd_ref, o_ref = refs
    tiles = _subtile_rows(h_ref.shape[0])
    live = [dict() for _ in tiles]

    def load(r):
        x = h_ref[tiles[r], :]
        if project_attention:
            x = x + _rms(_dot(attn_ref[tiles[r], :], wo_ref[...]), gmix_ref[...])
            o_ref[tiles[r], :] = x
        live[r]["xn"] = _rms(x, gpre_ref[...]).astype(_BF16)

    def expand(r):
        xn = live[r].pop("xn")
        g = _dot(xn, wg_ref[...])
        u = _dot(xn, wu_ref[...])
        live[r]["a"] = (g * jax.nn.sigmoid(g) * u).astype(_BF16)

    def contract(r):
        live[r]["f"] = _dot(live[r].pop("a"), wd_ref[...])

    def store(r):
        x = (o_ref if project_attention else h_ref)[tiles[r], :]
        o_ref[tiles[r], :] = x + 0.5 * _rms(live[r].pop("f"), gpost_ref[...])

    load(0)
    expand(0)
    for r in range(len(tiles)):
        if r + 1 < len(tiles):
            load(r + 1)
        contract(r)
        if r + 1 < len(tiles):
            expand(r + 1)
        store(r)


def _ffn(h, g_pre, g_post, wg, wu, wd, layer, attention=None):
    n, d = h.shape
    tm = min(TOKEN_TILE, n)
    tok = pl.BlockSpec((tm, d), lambda i: (i, 0))
    in_specs = [tok, _resident((1, d)), _resident((1, d)),
                _resident(wg.shape, layer), _resident(wu.shape, layer), _resident(wd.shape, layer)]
    args = [h, g_pre, g_post, wg, wu, wd]
    if attention is not None:
        attn_out, w_o, slab, g_mix = attention
        in_specs += [tok, _resident(w_o.shape, slab), _resident((1, d))]
        args += [attn_out, w_o, g_mix]
    return pl.pallas_call(
        functools.partial(_ffn_kernel, project_attention=attention is not None),
        grid=(n // tm,),
        in_specs=in_specs,
        out_specs=tok,
        out_shape=jax.ShapeDtypeStruct((n, d), _F32),
        compiler_params=_params("parallel"),
        name="ffn_attn_proj" if attention is not None else "ffn",
    )(*args)


def _pool_sgu_kernel(h_ref, gpre_ref, gpost_ref, win_ref, poolw_ref, pscale_ref, sgun_ref,
                     sguw_ref, sgubt_ref, wout_ref, o_ref, sums, ybuf):
    t = pl.program_id(1)
    tm = h_ref.shape[1]
    tiles = _subtile_rows(tm)
    ts = tm // len(tiles)
    gd = poolw_ref.shape[-1]
    pool_w = len(POOL_WINDOWS) * gd
    sgu_w = SGU_GROUPS * gd
    live = [dict() for _ in tiles]
    causal = (lax.broadcasted_iota(jnp.int32, (SGU_CHUNK, SGU_CHUNK), 0)
              >= lax.broadcasted_iota(jnp.int32, (SGU_CHUNK, SGU_CHUNK), 1))

    @pl.when(t == 0)
    def _():
        sums[:, 0:POOL_HALO, :] = jnp.zeros((sums.shape[0], POOL_HALO, pool_w), _F32)

    def project_in(r):
        z = _dot(_rms(h_ref[0, tiles[r], :], gpre_ref[...]).astype(_BF16), win_ref[...])
        sums[0, POOL_HALO + r * ts:POOL_HALO + (r + 1) * ts, :] = z[:, :pool_w]
        live[r].update(u=z[:, pool_w:pool_w + sgu_w], v=z[:, pool_w + sgu_w:])

    def mix(r):
        base = POOL_HALO + r * ts
        pos = t * tm + r * ts + lax.broadcasted_iota(jnp.int32, (ts, 1), 0)
        for g, win in enumerate(POOL_WINDOWS):
            lanes = slice(g * gd, (g + 1) * gd)
            a = sums[0, base:base + ts, lanes]
            s = a
            for k in range(win.bit_length() - 1):
                s = s + sums[k, base - 2 ** k:base - 2 ** k + ts, lanes]
                if 2 ** (k + 1) < win:
                    sums[k + 1, base:base + ts, lanes] = s
            inv_count = 1.0 / jnp.minimum(pos + 1, win).astype(_F32)
            d = (s * inv_count - a).astype(_BF16)
            ybuf[tiles[r], lanes] = (_dot(d, poolw_ref[g]) * pscale_ref[g:g + 1, :]).astype(_BF16)
        u = _gelu(live[r].pop("u"))
        v = _gelu(live[r].pop("v"))
        for g in range(SGU_GROUPS):
            lanes = slice(g * gd, (g + 1) * gd)
            vn = _rms(v[:, lanes], sgun_ref[g:g + 1, :]).astype(_BF16)
            w = jnp.where(causal, sguw_ref[g], 0.0).astype(_BF16)
            bias = sgubt_ref[:, g:g + 1]
            n_chunks = ts // SGU_CHUNK
            wide = jnp.concatenate([vn[c * SGU_CHUNK:(c + 1) * SGU_CHUNK] for c in range(n_chunks)], axis=1)
            s_wide = _dot(w, wide) + bias
            for c in range(n_chunks):
                rows = slice(c * SGU_CHUNK, (c + 1) * SGU_CHUNK)
                s = s_wide[:, c * gd:(c + 1) * gd]
                ybuf[r * ts + c * SGU_CHUNK:r * ts + (c + 1) * SGU_CHUNK,
                     pool_w + g * gd:pool_w + (g + 1) * gd] = (u[rows, lanes] * s).astype(_BF16)

    def project_out(r):
        m = _dot(ybuf[tiles[r], :], wout_ref[...])
        o_ref[0, tiles[r], :] = h_ref[0, tiles[r], :] + _rms(m, gpost_ref[...])

    project_in(0)
    for r in range(len(tiles)):
        if r + 1 < len(tiles):
            project_in(r + 1)
        mix(r)
        project_out(r)
    sums[:, 0:POOL_HALO, :] = sums[:, tm:tm + POOL_HALO, :]


def _pool_sgu(h, g_pre, g_post, w_in, pool_w, pool_scale, sgu_norm, sgu_w, sgu_bt, w_out, slab):
    b, s, d = h.shape
    tm = min(TOKEN_TILE, s)
    assert s % tm == 0 and min(SUBTILE, tm) % SGU_CHUNK == 0 and min(SUBTILE, tm) >= POOL_HALO
    g, gd = pool_scale.shape
    y_w = w_out.shape[1]
    assert all(w == 2 ** (w.bit_length() - 1) and w <= POOL_HALO for w in POOL_WINDOWS)
    n_sums = max(POOL_WINDOWS).bit_length() - 1
    tok = pl.BlockSpec((1, tm, d), lambda i, t: (i, t, 0))
    return pl.pallas_call(
        _pool_sgu_kernel,
        grid=(b, s // tm),
        in_specs=[tok, _resident((1, d)), _resident((1, d)), _resident(w_in.shape, slab),
                  _resident(pool_w.shape, slab), _resident(pool_scale.shape), _resident(sgu_norm.shape),
                  _resident(sgu_w.shape), _resident(sgu_bt.shape), _resident(w_out.shape, slab)],
        out_specs=tok,
        out_shape=jax.ShapeDtypeStruct((b, s, d), _F32),
        scratch_shapes=[pltpu.VMEM((n_sums, POOL_HALO + tm, g * gd), _F32), pltpu.VMEM((tm, y_w), _BF16)],
        compiler_params=_params("arbitrary", "arbitrary"),
        name="pool_sgu",
    )(h, g_pre, g_post, w_in, pool_w, pool_scale, sgu_norm, sgu_w, sgu_bt, w_out)


def _qkv_kernel(h_ref, gpre_ref, w_ref, cos_ref, sin_lo_ref, sin_hi_ref, q_ref, k_ref, vt_ref, km_ref):
    tm, d = h_ref.shape[1], h_ref.shape[2]
    hd = d // N_HEADS
    half = hd // ROT_DIM_FRACTION // 2
    tiles = _subtile_rows(tm)
    ts = tm // len(tiles)
    live = [dict() for _ in tiles]

    def project(r):
        live[r]["qkv"] = _dot(_rms(h_ref[0, tiles[r], :], gpre_ref[...]).astype(_BF16), w_ref[...])

    def emit(r):
        qkv = live[r].pop("qkv")
        cos, sin_lo, sin_hi = cos_ref[tiles[r], :], sin_lo_ref[tiles[r], :], sin_hi_ref[tiles[r], :]

        def rotary(xh):
            return (xh * cos + pltpu.roll(xh, hd - half, 1) * sin_lo + pltpu.roll(xh, half, 1) * sin_hi)

        for h in range(N_HEADS):
            lanes = slice(h * hd, (h + 1) * hd)
            q_ref[0, h, tiles[r], :] = rotary(qkv[:, h * hd:(h + 1) * hd])
            kr = rotary(qkv[:, d + h * hd:d + (h + 1) * hd])
            k_ref[0, h, tiles[r], :] = kr.astype(_BF16)
            vt_ref[0, h, :, tiles[r]] = qkv[:, 2 * d + h * hd:2 * d + (h + 1) * hd].T.astype(_BF16)
            for blk in range(ts // MOBA_BLOCK):
                rows = slice(blk * MOBA_BLOCK, (blk + 1) * MOBA_BLOCK)
                km_ref[0, r * (ts // MOBA_BLOCK) + blk, :, lanes] = jnp.mean(kr[rows], axis=0, keepdims=True)

    project(0)
    for r in range(len(tiles)):
        if r + 1 < len(tiles):
            project(r + 1)
        emit(r)


def _qkv(h, g_pre, w_qkv, slab, cos, sin_lo, sin_hi):
    b, s, d = h.shape
    hd = d // N_HEADS
    tm = min(QKV_TILE, s)
    assert s % tm == 0 and min(SUBTILE, tm) % MOBA_BLOCK == 0
    nb_tile = tm // MOBA_BLOCK
    tab = pl.BlockSpec((tm, hd), lambda i, t: (t, 0))
    return pl.pallas_call(
        _qkv_kernel,
        grid=(b, s // tm),
        in_specs=[pl.BlockSpec((1, tm, d), lambda i, t: (i, t, 0)), _resident((1, d)),
                  _resident(w_qkv.shape, slab), tab, tab, tab],
        out_specs=[pl.BlockSpec((1, N_HEADS, tm, hd), lambda i, t: (i, 0, t, 0)),
                   pl.BlockSpec((1, N_HEADS, tm, hd), lambda i, t: (i, 0, t, 0)),
                   pl.BlockSpec((1, N_HEADS, hd, tm), lambda i, t: (i, 0, 0, t)),
                   pl.BlockSpec((1, nb_tile, 1, d), lambda i, t: (i, t, 0, 0))],
        out_shape=[jax.ShapeDtypeStruct((b, N_HEADS, s, hd), _F32),
                   jax.ShapeDtypeStruct((b, N_HEADS, s, hd), _BF16),
                   jax.ShapeDtypeStruct((b, N_HEADS, hd, s), _BF16),
                   jax.ShapeDtypeStruct((b, s // MOBA_BLOCK, 1, d), _F32)],
        compiler_params=_params("parallel", "parallel"),
        name="qkv_rotary",
    )(h, g_pre, w_qkv, cos, sin_lo, sin_hi)


def _moba_kernel(q_ref, k_ref, vt_ref, km_ref, onehot_ref, o_ref, *t_refs):
    blk = MOBA_BLOCK
    sub = F32_SUBLANES
    km = km_ref[0]
    nb, hd = km.shape
    to_log2 = (hd ** -0.5) * LOG2_E
    causal = (lax.broadcasted_iota(jnp.int32, (blk, blk), 0) <= lax.broadcasted_iota(jnp.int32, (blk, blk), 1))
    state = [dict() for _ in range(nb)]

    def rows(j, n=1):
        return slice(j * blk, (j + n) * blk)

    def groups(c):
        return [(j, min(KV_GROUP, c + 1 - j)) for j in range(0, c + 1, KV_GROUP)]

    def gate_step(c):
        q32 = q_ref[0, 0, rows(c), :]
        gate = _dot_nt(km, q32, precision=lax.Precision.HIGHEST)
        bid = lax.broadcasted_iota(jnp.int32, gate.shape, 0)
        past = bid < c
        gate = jnp.where(past, gate, NEG_INF)
        rank = jnp.zeros(gate.shape, jnp.int32)
        for m in range(nb):
            row = gate[m:m + 1, :]
            rank = rank + ((row > gate) | ((row == gate) & (m < bid))).astype(jnp.int32)
        bias = jnp.where(past & (rank >= min(MOBA_TOPK, nb)), NEG_INF, 0.0)
        bias = jnp.concatenate([bias, jnp.zeros((hd - nb, blk), _F32)], axis=0).T
        qa = jnp.concatenate([(q32 * to_log2).astype(_BF16), bias.astype(_BF16)], axis=1)
        state[c].update(qa=qa, m=None, acc=None)

    def score_step(c, j, n):
        st = state[c]
        ka = jnp.concatenate([k_ref[0, 0, rows(j, n), :], onehot_ref[rows(j, n), :]], axis=1)
        s = _dot_nt(ka, st["qa"])
        for r in range(n):
            sr = s[r * blk:(r + 1) * blk]
            if j + r == c:
                sr = jnp.where(causal, sr, NEG_INF)
            t_refs[c % SCORE_BUFFERS][rows(j + r), :] = sr.astype(_BF16)
            part = jnp.max(sr.reshape(blk // sub, sub, blk), axis=0)
            st["m"] = part if st["m"] is None else jnp.maximum(st["m"], part)

    def softmax_step(c, j, n):
        st = state[c]
        if st["acc"] is None:
            st["m"] = jnp.max(st["m"], axis=0, keepdims=True)
        p = jnp.exp2(t_refs[c % SCORE_BUFFERS][rows(j, n), :] - st["m"].astype(_BF16))
        vta = jnp.concatenate([vt_ref[0, 0, :, rows(j, n)], jnp.ones((BF16_SUBLANES, n * blk), _BF16)], axis=0)
        pv = _dot(vta, p)
        st["acc"] = pv if st["acc"] is None else st["acc"] + pv

    def finish(c):
        acc = state[c]["acc"]
        inv_l = 1.0 / acc[hd:hd + 1, :]
        o_ref[0, rows(c), :] = (acc[0:hd] * inv_l).T.astype(o_ref.dtype)
        state[c].clear()

    done_before = lambda c: c * (c + 1) // 2
    schedule = []
    for c in range(nb):
        schedule.append((done_before(c) - GATE_LEAD, 0, ("gate", c), functools.partial(gate_step, c)))
        for j, n in groups(c):
            schedule.append((done_before(c) + j, 1, ("score", c), functools.partial(score_step, c, j, n)))
            schedule.append((done_before(c + 1) + SOFTMAX_LAG + j, 2, ("softmax", c),
                             functools.partial(softmax_step, c, j, n)))
        schedule.append((done_before(c + 1) + SOFTMAX_LAG + c + 0.5, 3, ("finish", c), functools.partial(finish, c)))
    schedule.sort(key=lambda e: e[:2])
    tags = [e[2] for e in schedule]
    for c in range(nb - SCORE_BUFFERS):
        last_read = max(i for i, tag in enumerate(tags) if tag == ("softmax", c))
        assert last_read < tags.index(("score", c + SCORE_BUFFERS))
    for *_, step in schedule:
        step()


def _moba(q, k, vt, kmean):
    b, nh, s, hd = q.shape
    nb = s // MOBA_BLOCK
    assert nb <= hd
    onehot = (jnp.arange(s)[:, None] // MOBA_BLOCK == jnp.arange(hd)[None, :]).astype(_BF16)
    return pl.pallas_call(
        _moba_kernel,
        grid=(b, nh),
        in_specs=[pl.BlockSpec((1, 1, s, hd), lambda i, h: (i, h, 0, 0)),
                  pl.BlockSpec((1, 1, s, hd), lambda i, h: (i, h, 0, 0)),
                  pl.BlockSpec((1, 1, hd, s), lambda i, h: (i, h, 0, 0)),
                  pl.BlockSpec((1, nb, hd), lambda i, h: (i, 0, h)), _resident((s, hd))],
        out_specs=pl.BlockSpec((1, s, hd), lambda i, h: (i, 0, h)),
        out_shape=jax.ShapeDtypeStruct((b, s, nh * hd), _BF16),
        scratch_shapes=[pltpu.VMEM((s, MOBA_BLOCK), _BF16)] * SCORE_BUFFERS,
        compiler_params=_params("parallel", "parallel"),
        name="moba_attention",
    )(q, k, vt, kmean, onehot)


def _rotary_tables(s, hd):
    rot = hd // ROT_DIM_FRACTION
    half = rot // 2
    pos = jnp.arange(s, dtype=_F32)
    inv_freq = 1.0 / (ROPE_THETA ** (jnp.arange(0, rot, 2, dtype=_F32) / rot))
    ang = pos[:, None] * inv_freq[None, :]
    cos, sin = jnp.cos(ang), jnp.sin(ang)
    zeros = jnp.zeros((s, hd - rot), _F32)
    zhalf = jnp.zeros((s, half), _F32)
    return (jnp.concatenate([cos, cos, jnp.ones((s, hd - rot), _F32)], axis=1),
            jnp.concatenate([-sin, zhalf, zeros], axis=1),
            jnp.concatenate([zhalf, sin, zeros], axis=1))


def kernel(x, ffn1_w_gate, ffn1_w_up, ffn1_w_down, ffn2_w_gate, ffn2_w_up, ffn2_w_down, norm_pre, norm_post, ab_w_in, pool_w, pool_scale, sgu_norm, sgu_w, sgu_b, ab_w_out, attn_w_qkv, attn_w_o):
    b, s, d = x.shape
    depth = ffn1_w_gate.shape[0]
    ffn1 = [w.astype(_BF16) for w in (ffn1_w_gate, ffn1_w_up, ffn1_w_down)]
    ffn2 = [w.astype(_BF16) for w in (ffn2_w_gate, ffn2_w_up, ffn2_w_down)]
    ab_w_in, pool_w, ab_w_out, attn_w_qkv, attn_w_o = (
        w.astype(_BF16) for w in (ab_w_in, pool_w, ab_w_out, attn_w_qkv, attn_w_o))
    tables = _rotary_tables(s, d // N_HEADS)
    h = x.reshape(b * s, d)
    for layer in range(depth):
        pre, post = norm_pre[layer], norm_post[layer]
        h = _ffn(h, pre[0:1], post[0:1], *ffn1, layer)
        i = layer // 2
        attention = None
        if layer % 2 == 0:
            h = _pool_sgu(h.reshape(b, s, d), pre[1:2], post[1:2], ab_w_in, pool_w, pool_scale[i],
                          sgu_norm[i], sgu_w[i], sgu_b[i].T, ab_w_out, i).reshape(b * s, d)
        else:
            q, k, vt, kmean = _qkv(h.reshape(b, s, d), pre[1:2], attn_w_qkv, i, *tables)
            o = _moba(q, k, vt, kmean.reshape(b, s // MOBA_BLOCK, d))
            attention = (o.reshape(b * s, d), attn_w_o, i, post[1:2])
        h = _ffn(h, pre[2:3], post[2:3], *ffn2, layer, attention=attention)
    return h.reshape(b, s, d)
```
